```python
import math
import jax, jax.numpy as jnp
from jax import lax
import numpy as np

D_MODEL = 4096
BATCH = 2
SEQ = 8192
DEPTH = 2

N_A_LAYERS = DEPTH // 2
N_B_LAYERS = DEPTH - N_A_LAYERS

ALPHA = (2.0 * DEPTH) ** 0.25
BETA = (8.0 * DEPTH) ** -0.25
LN_EPS = 1e-5
RMS_EPS = 1e-6

HGRN_EXPAND = 128
HGRN_HEADS = D_MODEL // HGRN_EXPAND
HGRN_DK = HGRN_EXPAND
HGRN_DV = D_MODEL // HGRN_HEADS
HGRN_CHUNK = 32

HEAD_DIM = 64
N_HEADS = D_MODEL // HEAD_DIM
KV_HEADS = 8
GROUP = N_HEADS // KV_HEADS
D_ATTN = N_HEADS * HEAD_DIM
WINDOW = 128
ATTN_BLOCK = 128
ATTN_SCALE = HEAD_DIM ** -0.5
NEG_INF = -1e30

PEER_HEADS = 8
PEER_NKEYS = 128
PEER_EXPERTS = PEER_NKEYS * PEER_NKEYS
PEER_TOPK = 16
PEER_QDIM = 256
PEER_HALF = PEER_QDIM // 2
PEER_TOKEN_BLOCK = 128

kernel_name = 'hgrn2_swa_sink_peer_yoco'


def layer_norm(x, g, b):
    xf = x.astype(jnp.float32)
    mu = jnp.mean(xf, axis=-1, keepdims=True)
    xc = xf - mu
    var = jnp.mean(xc * xc, axis=-1, keepdims=True)
    y = xc * lax.rsqrt(var + LN_EPS) * g.astype(jnp.float32) + b.astype(jnp.float32)
    return y.astype(x.dtype)


def alibi_slopes(n_heads):
    return jnp.asarray(2.0 ** (-8.0 * np.arange(1, n_heads + 1) / n_heads), dtype=jnp.float32)


def hgrn2_mixer(x, w_in, lb, norm_gain, w_out):
    bsz, seq, _ = x.shape
    n = seq // HGRN_CHUNK
    proj = x @ w_in
    q_raw, f_raw, i_raw, g_raw = jnp.split(proj, 4, axis=-1)

    def heads(t):
        return t.reshape(bsz, n, HGRN_CHUNK, HGRN_HEADS, -1).transpose(0, 1, 3, 2, 4)

    f = lb + (1.0 - lb) * jax.nn.sigmoid(f_raw.astype(jnp.float32))
    q = heads(jax.nn.silu(q_raw.astype(jnp.float32)))
    k = heads(1.0 - f)
    v = heads(i_raw.astype(jnp.float32))
    b = jnp.cumsum(heads(jnp.log(f)), axis=3)
    b_last = b[:, :, :, -1:, :]
    qg = q * jnp.exp(b)
    kg = k * jnp.exp(-b)
    kd = k * jnp.exp(b_last - b)
    dec = jnp.exp(b_last[:, :, :, 0, :])

    causal = jnp.tril(jnp.ones((HGRN_CHUNK, HGRN_CHUNK), dtype=bool))
    a = jnp.where(causal, jnp.einsum('bnhcd,bnhsd->bnhcs', qg, kg), 0.0)
    o_intra = jnp.einsum('bnhcs,bnhse->bnhce', a, v)

    def chunk_step(state, inp):
        qg_c, kd_c, v_c, dec_c = inp
        o_c = jnp.einsum('bhcd,bhde->bhce', qg_c, state)
        state = dec_c[..., None] * state + jnp.einsum('bhcd,bhce->bhde', kd_c, v_c)
        return state, o_c

    state0 = jnp.zeros((bsz, HGRN_HEADS, HGRN_DK, HGRN_DV), jnp.float32)
    xs = (jnp.moveaxis(qg, 1, 0), jnp.moveaxis(kd, 1, 0), jnp.moveaxis(v, 1, 0), jnp.moveaxis(dec, 1, 0))
    _, o_inter = lax.scan(chunk_step, state0, xs)
    o = o_intra + jnp.moveaxis(o_inter, 0, 1)
    o = o.transpose(0, 1, 3, 2, 4).reshape(bsz, seq, HGRN_HEADS, HGRN_DV)
    o = o * lax.rsqrt(jnp.mean(o * o, axis=-1, keepdims=True) + RMS_EPS) * norm_gain.astype(jnp.float32)
    o = o.reshape(bsz, seq, D_MODEL) * jax.nn.silu(g_raw.astype(jnp.float32))
    return o.astype(x.dtype) @ w_out


def shared_kv(x, kv_w):
    bsz, seq, _ = x.shape
    kv = x @ kv_w
    k, v = jnp.split(kv, 2, axis=-1)
    return (k.reshape(bsz, seq, KV_HEADS, HEAD_DIM), v.reshape(bsz, seq, KV_HEADS, HEAD_DIM))


def swa_sink_attention(x, w_q, sinks, k_sh, v_sh, w_out):
    bsz, seq, _ = x.shape
    nb = seq // ATTN_BLOCK
    q = (x @ w_q).reshape(bsz, nb, ATTN_BLOCK, KV_HEADS, GROUP, HEAD_DIM)

    def kv_blocks(t):
        tp = jnp.pad(t, ((0, 0), (ATTN_BLOCK, 0), (0, 0), (0, 0)))
        prev = tp[:, :seq].reshape(bsz, nb, ATTN_BLOCK, KV_HEADS, HEAD_DIM)
        cur = t.reshape(bsz, nb, ATTN_BLOCK, KV_HEADS, HEAD_DIM)
        return jnp.concatenate([prev, cur], axis=2)

    kb = kv_blocks(k_sh)
    vb = kv_blocks(v_sh)
    s = jnp.einsum('bnqkgd,bnskd->bnkgqs', q, kb).astype(jnp.float32) * ATTN_SCALE

    qi = jnp.arange(ATTN_BLOCK)[:, None]
    si = jnp.arange(2 * ATTN_BLOCK)[None, :]
    dist = qi + ATTN_BLOCK - si
    key_pos = jnp.arange(nb)[:, None] * ATTN_BLOCK - ATTN_BLOCK + jnp.arange(2 * ATTN_BLOCK)[None, :]
    mask = ((dist >= 0) & (dist < WINDOW))[None] & (key_pos >= 0)[:, None, :]

    slopes = alibi_slopes(N_HEADS).reshape(KV_HEADS, GROUP)
    s = s - slopes[:, :, None, None] * dist.astype(jnp.float32)
    s = jnp.where(mask[None, :, None, None], s, NEG_INF)

    sink = sinks.astype(jnp.float32).reshape(KV_HEADS, GROUP)[None, None, :, :, None, None]
    m = jnp.maximum(jnp.max(s, axis=-1, keepdims=True), sink)
    p = jnp.exp(s - m)
    w = p / (jnp.sum(p, axis=-1, keepdims=True) + jnp.exp(sink - m))
    o = jnp.einsum('bnkgqs,bnskd->bnqkgd', w.astype(x.dtype), vb).reshape(bsz, seq, D_ATTN)
    return o @ w_out


def peer_ffn(x, w_q, subkeys, u_tab, v_tab):
    bsz, seq, d = x.shape
    t = bsz * seq
    xt = x.reshape(t, d)
    q = (xt @ w_q).reshape(t, PEER_HEADS, 2, PEER_HALF).astype(jnp.float32)
    sc = jnp.einsum('thpd,hpkd->thpk', q, subkeys.astype(jnp.float32))
    top_s, top_i = lax.top_k(sc, PEER_TOPK)
    cand_s = (top_s[:, :, 0, :, None] + top_s[:, :, 1, None, :]).reshape(t, PEER_HEADS, PEER_TOPK * PEER_TOPK)
    cand_i = (top_i[:, :, 0, :, None] * PEER_NKEYS + top_i[:, :, 1, None, :]).reshape(t, PEER_HEADS, PEER_TOPK * PEER_TOPK)
    best_s, pos = lax.top_k(cand_s, PEER_TOPK)
    idx = jnp.take_along_axis(cand_i, pos, axis=-1)
    gate = jax.nn.softmax(best_s, axis=-1)

    n_sel = PEER_HEADS * PEER_TOPK
    nblk = t // PEER_TOKEN_BLOCK
    xs = (xt.reshape(nblk, PEER_TOKEN_BLOCK, d),
          idx.reshape(nblk, PEER_TOKEN_BLOCK, n_sel),
          gate.reshape(nblk, PEER_TOKEN_BLOCK, n_sel))

    def expert_block(args):
        xb, ib, gb = args
        u = jnp.take(u_tab, ib, axis=0)
        h = jnp.einsum('td,ted->te', xb, u).astype(jnp.float32)
        a = (jax.nn.gelu(h, approximate=False) * gb).astype(xb.dtype)
        vsel = jnp.take(v_tab, ib, axis=0)
        return jnp.einsum('te,ted->td', a, vsel)

    return lax.map(expert_block, xs).reshape(bsz, seq, d)


def setup_inputs(seed: int = 0) -> dict:
    key = jax.random.key(seed)
    ks = jax.random.split(key, 16)
    f32 = jnp.float32
    sd = D_MODEL ** -0.5
    nrm = lambda k, shape: jax.random.normal(k, shape, f32)
    in_scale = jnp.concatenate([jnp.full((D_MODEL,), sd, f32), jnp.full((D_MODEL,), sd, f32),
                                jnp.full((D_MODEL,), BETA * sd, f32), jnp.full((D_MODEL,), sd, f32)])
    kv_scale = jnp.concatenate([jnp.full((KV_HEADS * HEAD_DIM,), sd, f32),
                                jnp.full((KV_HEADS * HEAD_DIM,), BETA * sd, f32)])
    return {
        'x': nrm(ks[0], (BATCH, SEQ, D_MODEL)),
        'hgrn_w_in': nrm(ks[1], (N_A_LAYERS, D_MODEL, 4 * D_MODEL)) * in_scale,
        'hgrn_lb_logits': 0.1 * nrm(ks[2], (N_A_LAYERS + 1, D_MODEL)),
        'hgrn_norm_gain': 1.0 + 0.02 * nrm(ks[3], (N_A_LAYERS, HGRN_DV)),
        'hgrn_w_out': nrm(ks[4], (N_A_LAYERS, D_MODEL, D_MODEL)) * (BETA * sd),
        'kv_w': nrm(ks[5], (D_MODEL, 2 * KV_HEADS * HEAD_DIM)) * kv_scale,
        'attn_w_q': nrm(ks[6], (N_B_LAYERS, D_MODEL, D_ATTN)) * sd,
        'attn_sinks': 0.5 * nrm(ks[7], (N_B_LAYERS, N_HEADS)),
        'attn_w_out': nrm(ks[8], (N_B_LAYERS, D_ATTN, D_MODEL)) * (BETA * D_ATTN ** -0.5),
        'peer_w_q': nrm(ks[9], (DEPTH, D_MODEL, PEER_HEADS * PEER_QDIM)) * sd,
        'peer_subkeys': nrm(ks[10], (DEPTH, PEER_HEADS, 2, PEER_NKEYS, PEER_HALF)) * (PEER_HALF ** -0.5),
        'peer_u': nrm(ks[11], (DEPTH, PEER_EXPERTS, D_MODEL)) * sd,
        'peer_v': nrm(ks[12], (DEPTH, PEER_EXPERTS, D_MODEL)) * BETA,
        'ln_gain': 1.0 + 0.02 * nrm(ks[13], (DEPTH, 2, D_MODEL)),
        'ln_bias': 0.02 * nrm(ks[14], (DEPTH, 2, D_MODEL)),
    }


def reference(x, hgrn_w_in, hgrn_lb_logits, hgrn_norm_gain, hgrn_w_out, kv_w, attn_w_q, attn_sinks,
              attn_w_out, peer_w_q, peer_subkeys, peer_u, peer_v, ln_gain, ln_bias):
    lb_all = jnp.cumsum(jax.nn.softmax(hgrn_lb_logits.astype(jnp.float32), axis=0), axis=0)
    k_sh = None
    v_sh = None
    for l in range(DEPTH):
        if l < N_A_LAYERS:
            h = hgrn2_mixer(x, hgrn_w_in[l], lb_all[l], hgrn_norm_gain[l], hgrn_w_out[l])
        else:
            j = l - N_A_LAYERS
            h = swa_sink_attention(x, attn_w_q[j], attn_sinks[j], k_sh, v_sh, attn_w_out[j])
        x = layer_norm(ALPHA * x + h, ln_gain[l, 0], ln_bias[l, 0])
        x = layer_norm(ALPHA * x + peer_ffn(x, peer_w_q[l], peer_subkeys[l], peer_u[l], peer_v[l]),
                       ln_gain[l, 1], ln_bias[l, 1])
        if l == N_A_LAYERS - 1:
            k_sh, v_sh = shared_kv(x, kv_w)
    return x
```

```python
import functools
import math

import jax
import jax.numpy as jnp
from jax import lax
from jax.experimental import pallas as pl
from jax.experimental.pallas import tpu as pltpu

F32 = jnp.float32
MXU_DTYPE = jnp.bfloat16

DEPTH = 2
ALPHA = (2.0 * DEPTH) ** 0.25
LN_EPS = 1e-5
RMS_EPS = 1e-6

HGRN_DK = 128
HGRN_CHUNK = 32

HEAD_DIM = 64
KV_HEADS = 8
ATTN_BLOCK = 128
ATTN_SCALE = HEAD_DIM ** -0.5
NEG_INF = -1e30

PEER_HEADS = 8
PEER_NKEYS = 128
PEER_TOPK = 16
PEER_HALF = 128

LANES = 128
SUBLANES = 8
VMEM_LIMIT = 56 * 1024 * 1024


def _params(sem):
    return pltpu.CompilerParams(dimension_semantics=sem, vmem_limit_bytes=VMEM_LIMIT)


def _sigmoid(x):
    return 1.0 / (1.0 + jnp.exp(-x))


def _dot(a, b):
    return jnp.dot(a, b, preferred_element_type=F32)


def _dot_nt(a, b):
    return lax.dot_general(a, b, (((1,), (1,)), ((), ())), preferred_element_type=F32)


def _dot_tn(a, b):
    return lax.dot_general(a, b, (((0,), (0,)), ((), ())), preferred_element_type=F32)


def _mm_kernel(a_ref, w_ref, o_ref):
    a = a_ref[...].astype(MXU_DTYPE)
    o_ref[...] = _dot(a, w_ref[...]).astype(o_ref.dtype)


def _matmul(a, w, out_dtype, tm, tn):
    m, k = a.shape
    n = w.shape[1]
    tm, tn = min(tm, m), min(tn, n)
    return pl.pallas_call(
        _mm_kernel,
        grid=(m // tm, n // tn),
        in_specs=[pl.BlockSpec((tm, k), lambda i, j: (i, 0)),
                  pl.BlockSpec((k, tn), lambda i, j: (0, j))],
        out_specs=pl.BlockSpec((tm, tn), lambda i, j: (i, j)),
        out_shape=jax.ShapeDtypeStruct((m, n), out_dtype),
        compiler_params=_params(("parallel", "arbitrary")),
        name="matmul",
    )(a, w)


def _ln_rows(z, g, b):
    mu = jnp.mean(z, axis=-1, keepdims=True)
    zc = z - mu
    var = jnp.mean(zc * zc, axis=-1, keepdims=True)
    return zc * lax.rsqrt(var + LN_EPS) * g + b


def _mm_ln_kernel(a_ref, w_ref, res_ref, g_ref, b_ref, o_ref, *, tn, nj, rows):
    j = pl.program_id(1)
    y = _dot(a_ref[...].astype(MXU_DTYPE), w_ref[...]) + ALPHA * res_ref[...]
    for jj in range(nj):
        @pl.when(j == jj)
        def _(jj=jj):
            o_ref[:, jj * tn:(jj + 1) * tn] = y

    @pl.when(j == nj - 1)
    def _():
        g = g_ref[...]
        b = b_ref[...]

        def body(r, carry):
            sl = pl.ds(pl.multiple_of(r * rows, rows), rows)
            o_ref[sl, :] = _ln_rows(o_ref[sl, :], g, b)
            return carry

        lax.fori_loop(0, o_ref.shape[0] // rows, body, 0)


def _matmul_ln(a, w, res, gain, bias, tm, tn):
    m, k = a.shape
    n = w.shape[1]
    tm, tn = min(tm, m), min(tn, n)
    nj = n // tn
    return pl.pallas_call(
        functools.partial(_mm_ln_kernel, tn=tn, nj=nj, rows=2 * SUBLANES),
        grid=(m // tm, nj),
        in_specs=[pl.BlockSpec((tm, k), lambda i, j: (i, 0)),
                  pl.BlockSpec((k, tn), lambda i, j: (0, j)),
                  pl.BlockSpec((tm, tn), lambda i, j: (i, j)),
                  pl.BlockSpec((1, n), lambda i, j: (0, 0)),
                  pl.BlockSpec((1, n), lambda i, j: (0, 0))],
        out_specs=pl.BlockSpec((tm, n), lambda i, j: (i, 0)),
        out_shape=jax.ShapeDtypeStruct((m, n), F32),
        compiler_params=_params(("parallel", "arbitrary")),
        name="matmul_ln",
    )(a, w, res, gain.reshape(1, n), bias.reshape(1, n))


def _res_ln_kernel(x_ref, y_ref, g_ref, b_ref, o_ref):
    o_ref[...] = _ln_rows(ALPHA * x_ref[...] + y_ref[...], g_ref[...], b_ref[...])


def _res_ln(x, y, gain, bias, tm):
    m, n = x.shape
    tm = min(tm, m)
    row = pl.BlockSpec((tm, n), lambda i: (i, 0))
    vec = pl.BlockSpec((1, n), lambda i: (0, 0))
    return pl.pallas_call(
        _res_ln_kernel,
        grid=(m // tm,),
        in_specs=[row, row, vec, vec],
        out_specs=row,
        out_shape=jax.ShapeDtypeStruct((m, n), F32),
        compiler_params=_params(("parallel",)),
        name="res_ln",
    )(x, y, gain.reshape(1, n), bias.reshape(1, n))


def _hgrn_kernel(x_ref, w_ref, lbl_ref, gain_ref, o_ref,
                 state_ref, qg_ref, kg_ref, k_ref, b_ref, v_ref, g_ref, *, layer, ts):
    sblk = pl.program_id(1)
    h = pl.program_id(2)
    dk = HGRN_DK
    c = HGRN_CHUNK

    @pl.when(sblk == 0)
    def _():
        state_ref[h] = jnp.zeros((dk, dk), F32)

    lg = lbl_ref[0]
    ex = jnp.exp(lg - jnp.max(lg, axis=0, keepdims=True))
    lb = jnp.sum(ex[:layer + 1], axis=0, keepdims=True) / jnp.sum(ex, axis=0, keepdims=True)

    proj = _dot(x_ref[0], w_ref[0])
    q_raw = proj[:, 0 * dk:1 * dk]
    f_raw = proj[:, 1 * dk:2 * dk]
    f = lb + (1.0 - lb) * _sigmoid(f_raw)
    k = 1.0 - f
    b = jnp.log(f)
    row = lax.broadcasted_iota(jnp.int32, (ts, dk), 0) % c
    sh = 1
    while sh < c:
        b = b + jnp.where(row >= sh, pltpu.roll(b, sh, axis=0), 0.0)
        sh *= 2
    qg_ref[...] = q_raw * _sigmoid(q_raw) * jnp.exp(b)
    kg_ref[...] = k * jnp.exp(-b)
    k_ref[...] = k
    b_ref[...] = b
    v_ref[...] = proj[:, 2 * dk:3 * dk]
    g_ref[...] = proj[:, 3 * dk:4 * dk]

    causal = (lax.broadcasted_iota(jnp.int32, (c, c), 0)
              >= lax.broadcasted_iota(jnp.int32, (c, c), 1))
    gain = gain_ref[...]

    def chunk(ci, st):
        sl = pl.ds(pl.multiple_of(ci * c, c), c)
        qg = qg_ref[sl, :].astype(MXU_DTYPE)
        vv = v_ref[sl, :].astype(MXU_DTYPE)
        bc = b_ref[sl, :]
        bl = bc[c - 1:c, :]
        a = jnp.where(causal, _dot_nt(qg, kg_ref[sl, :].astype(MXU_DTYPE)), 0.0)
        o = _dot(a.astype(MXU_DTYPE), vv) + _dot_nt(qg, st.astype(MXU_DTYPE))
        kd = (k_ref[sl, :] * jnp.exp(bl - bc)).astype(MXU_DTYPE)
        st = st * jnp.exp(bl) + _dot_tn(vv, kd)
        o = o * lax.rsqrt(jnp.mean(o * o, axis=-1, keepdims=True) + RMS_EPS) * gain
        gr = g_ref[sl, :]
        o_ref[0, sl, :] = (o * (gr * _sigmoid(gr))).astype(o_ref.dtype)
        return st

    state_ref[h] = lax.fori_loop(0, ts // c, chunk, state_ref[h])


def _hgrn(xb, w_heads, lb_logits, gain, layer, ts):
    bsz, seq, d = xb.shape
    nh = w_heads.shape[0]
    dk = HGRN_DK
    ts = min(ts, seq)
    slots = lb_logits.shape[0]
    lbl = lb_logits.reshape(slots, nh, dk).transpose(1, 0, 2)
    buf = pltpu.VMEM((ts, dk), F32)
    return pl.pallas_call(
        functools.partial(_hgrn_kernel, layer=layer, ts=ts),
        grid=(bsz, seq // ts, nh),
        in_specs=[pl.BlockSpec((1, ts, d), lambda b, s, h: (b, s, 0)),
                  pl.BlockSpec((1, d, 4 * dk), lambda b, s, h: (h, 0, 0)),
                  pl.BlockSpec((1, slots, dk), lambda b, s, h: (h, 0, 0)),
                  pl.BlockSpec((1, dk), lambda b, s, h: (0, 0))],
        out_specs=pl.BlockSpec((1, ts, dk), lambda b, s, h: (b, s, h)),
        out_shape=jax.ShapeDtypeStruct((bsz, seq, d), MXU_DTYPE),
        scratch_shapes=[pltpu.VMEM((nh, dk, dk), F32), buf, buf, buf, buf, buf, buf],
        compiler_params=_params(("parallel", "arbitrary", "arbitrary")),
        name="hgrn2",
    )(xb, w_heads, lbl, gain.reshape(1, dk))


def _attn_kernel(q_ref, kp_ref, kc_ref, vp_ref, vc_ref, sink_ref, slope_ref, o_ref, *, group):
    n = pl.program_id(1)
    kvh = pl.program_id(2)
    blk = ATTN_BLOCK
    kk = jnp.concatenate([kp_ref[0, 0], kc_ref[0, 0]], axis=0)
    vv = jnp.concatenate([vp_ref[0, 0], vc_ref[0, 0]], axis=0)
    qi = lax.broadcasted_iota(jnp.int32, (blk, 2 * blk), 0)
    si = lax.broadcasted_iota(jnp.int32, (blk, 2 * blk), 1)
    dist = qi + blk - si
    mask = (dist >= 0) & (dist < blk) & ((si >= blk) | (n > 0))
    distf = dist.astype(F32)
    outs = []
    for g in range(group):
        slope = slope_ref[kvh * group + g]
        qh = q_ref[0, :, g * HEAD_DIM:(g + 1) * HEAD_DIM]
        s = _dot_nt(qh, kk) * ATTN_SCALE - slope * distf
        s = jnp.where(mask, s, NEG_INF)
        sink = sink_ref[kvh * group + g]
        m = jnp.maximum(jnp.max(s, axis=-1, keepdims=True), sink)
        p = jnp.exp(s - m)
        w = p / (jnp.sum(p, axis=-1, keepdims=True) + jnp.exp(sink - m))
        outs.append(_dot(w.astype(MXU_DTYPE), vv))
    o_ref[0] = jnp.concatenate(outs, axis=-1).astype(o_ref.dtype)


def _attention(q, k_sh, v_sh, sinks):
    bsz, seq, dq = q.shape
    group = dq // (KV_HEADS * HEAD_DIM)
    blk = ATTN_BLOCK
    gw = group * HEAD_DIM
    cur = pl.BlockSpec((1, 1, blk, HEAD_DIM), lambda b, n, k: (b, k, n, 0))
    prev = pl.BlockSpec((1, 1, blk, HEAD_DIM), lambda b, n, k: (b, k, jnp.maximum(n - 1, 0), 0))
    nheads = KV_HEADS * group
    slopes = jnp.asarray([2.0 ** (-8.0 * h / nheads) for h in range(1, nheads + 1)], F32)
    return pl.pallas_call(
        functools.partial(_attn_kernel, group=group),
        grid=(bsz, seq // blk, KV_HEADS),
        in_specs=[pl.BlockSpec((1, blk, gw), lambda b, n, k: (b, n, k)),
                  prev, cur, prev, cur,
                  pl.BlockSpec(memory_space=pltpu.SMEM),
                  pl.BlockSpec(memory_space=pltpu.SMEM)],
        out_specs=pl.BlockSpec((1, blk, gw), lambda b, n, k: (b, n, k)),
        out_shape=jax.ShapeDtypeStruct((bsz, seq, dq), MXU_DTYPE),
        compiler_params=_params(("parallel", "parallel", "arbitrary")),
        name="swa_attention",
    )(q, k_sh, k_sh, v_sh, v_sh, sinks, slopes)


def _top16(s):
    rid = lax.broadcasted_iota(jnp.int32, (PEER_TOPK, s.shape[1]), 0)
    tops = jnp.zeros((PEER_TOPK, s.shape[1]), F32)
    work = s
    for kth in range(PEER_TOPK):
        m = jnp.max(work, axis=0, keepdims=True)
        tops = jnp.where(rid == kth, m, tops)
        work = jnp.where(work == m, -jnp.inf, work)
    return tops


def _peer_sel_kernel(x_ref, wq_ref, sub_ref, s1_ref, s2_ref, st_ref, *, tb):
    q = _dot(x_ref[...].astype(MXU_DTYPE), wq_ref[...]).astype(MXU_DTYPE)
    s1 = _dot_nt(sub_ref[0, 0], q[:, :PEER_HALF])
    s2 = _dot_nt(sub_ref[0, 1], q[:, PEER_HALF:])
    for lb in range(tb // LANES):
        a1 = s1[:, lb * LANES:(lb + 1) * LANES]
        a2 = s2[:, lb * LANES:(lb + 1) * LANES]
        s1_ref[lb, 0] = a1
        s2_ref[lb, 0] = a2
        t1 = _top16(a1)
        t2 = _top16(a2)
        cands = [t1[0:1] + t2]
        for a in range(1, SUBLANES):
            cands.append(t1[a:a + 1] + t2[0:SUBLANES])
        cands.append(t1[SUBLANES:] + t2[0:1])
        cand = jnp.concatenate(cands, axis=0)
        best = _top16(cand)
        top = best[0:1]
        tau = best[PEER_TOPK - 1:PEER_TOPK]
        z = jnp.sum(jnp.where(cand >= tau, jnp.exp(cand - top), 0.0), axis=0, keepdims=True)
        zero = jnp.zeros_like(z)
        st_ref[lb, 0] = jnp.concatenate(
            [tau, t1[0:1], t2[0:1], 1.0 / z, zero, zero, zero, zero], axis=0)


def _peer_select(x, w_q, subkeys, tb):
    t, d = x.shape
    tb = min(tb, t)
    nlb = tb // LANES
    qd = 2 * PEER_HALF
    blk = lambda rows: pl.BlockSpec((nlb, 1, rows, LANES), lambda i, h: (i, h, 0, 0))
    shp = lambda rows: jax.ShapeDtypeStruct((t // LANES, PEER_HEADS, rows, LANES), F32)
    return pl.pallas_call(
        functools.partial(_peer_sel_kernel, tb=tb),
        grid=(t // tb, PEER_HEADS),
        in_specs=[pl.BlockSpec((tb, d), lambda i, h: (i, 0)),
                  pl.BlockSpec((d, qd), lambda i, h: (0, h)),
                  pl.BlockSpec((1, 2, PEER_NKEYS, PEER_HALF), lambda i, h: (h, 0, 0, 0))],
        out_specs=[blk(PEER_NKEYS), blk(PEER_NKEYS), blk(SUBLANES)],
        out_shape=[shp(PEER_NKEYS), shp(PEER_NKEYS), shp(SUBLANES)],
        compiler_params=_params(("parallel", "arbitrary")),
        name="peer_select",
    )(x, w_q, subkeys)


def _gelu(x):
    return 0.5 * x * (1.0 + lax.erf(x * (1.0 / math.sqrt(2.0))))


def _peer_dense_kernel(x_ref, u_ref, v_ref, s1_ref, s2_ref, st_ref, o_ref,
                       e2_ref, w1_ref, ht_ref, at_ref, *, tb, ec):
    e = pl.program_id(1)
    nlb = tb // LANES
    nsub = ec // PEER_NKEYS

    @pl.when(e == 0)
    def _():
        o_ref[...] = jnp.zeros(o_ref.shape, F32)
        for lb in range(nlb):
            for h in range(PEER_HEADS):
                st = st_ref[lb, h]
                e2_ref[lb, h] = jnp.exp(s2_ref[lb, h] - st[2:3])
                w1_ref[lb, h] = jnp.exp(s1_ref[lb, h] - st[1:2]) * st[3:4]

    ht_ref[...] = _dot_nt(u_ref[...], x_ref[...].astype(MXU_DTYPE))

    def sub(ii, carry):
        i = e * nsub + ii
        rows = pl.ds(pl.multiple_of(ii * PEER_NKEYS, PEER_NKEYS), PEER_NKEYS)
        for lb in range(nlb):
            lanes = slice(lb * LANES, (lb + 1) * LANES)
            gate = jnp.zeros((PEER_NKEYS, LANES), F32)
            for h in range(PEER_HEADS):
                tau = st_ref[lb, h, 0:1, :]
                cs = s1_ref[lb, h, pl.ds(i, 1), :] + s2_ref[lb, h]
                val = e2_ref[lb, h] * w1_ref[lb, h, pl.ds(i, 1), :]
                gate = gate + jnp.where(cs >= tau, val, 0.0)
            at_ref[rows, lanes] = (_gelu(ht_ref[rows, lanes]) * gate).astype(at_ref.dtype)
        return carry

    lax.fori_loop(0, nsub, sub, 0)
    o_ref[...] += _dot_tn(at_ref[...], v_ref[...])


def _peer_dense(x, u, v, s1, s2, st, tb, ec):
    t, d = x.shape
    tb = min(tb, t)
    nlb = tb // LANES
    n_exp = u.shape[0]
    sblk = lambda rows: pl.BlockSpec((nlb, PEER_HEADS, rows, LANES), lambda i, e: (i, 0, 0, 0))
    return pl.pallas_call(
        functools.partial(_peer_dense_kernel, tb=tb, ec=ec),
        grid=(t // tb, n_exp // ec),
        in_specs=[pl.BlockSpec((tb, d), lambda i, e: (i, 0)),
                  pl.BlockSpec((ec, d), lambda i, e: (e, 0)),
                  pl.BlockSpec((ec, d), lambda i, e: (e, 0)),
                  sblk(PEER_NKEYS), sblk(PEER_NKEYS), sblk(SUBLANES)],
        out_specs=pl.BlockSpec((tb, d), lambda i, e: (i, 0)),
        out_shape=jax.ShapeDtypeStruct((t, d), F32),
        scratch_shapes=[pltpu.VMEM((nlb, PEER_HEADS, PEER_NKEYS, LANES), F32),
                        pltpu.VMEM((nlb, PEER_HEADS, PEER_NKEYS, LANES), F32),
                        pltpu.VMEM((ec, tb), F32),
                        pltpu.VMEM((ec, tb), MXU_DTYPE)],
        compiler_params=_params(("parallel", "arbitrary")),
        name="peer_dense",
    )(x, u, v, s1, s2, st)


def _peer_layer(x, xb, w_q, subkeys, u, v, gain, bias):
    s1, s2, st = _peer_select(xb, w_q, subkeys, tb=512)
    y = _peer_dense(xb, u, v, s1, s2, st, tb=512, ec=512)
    return _res_ln(x, y, gain, bias, tm=256)


def kernel(x, hgrn_w_in, hgrn_lb_logits, hgrn_norm_gain, hgrn_w_out, kv_w, attn_w_q, attn_sinks,
           attn_w_out, peer_w_q, peer_subkeys, peer_u, peer_v, ln_gain, ln_bias):
    bsz, seq, d = x.shape
    t = bsz * seq
    cdt = MXU_DTYPE
    dk = HGRN_DK
    nh = d // dk

    w_in = hgrn_w_in[0].astype(cdt).reshape(d, 4, nh, dk).transpose(2, 0, 1, 3).reshape(nh, d, 4 * dk)
    o = _hgrn(x.astype(cdt), w_in, hgrn_lb_logits, hgrn_norm_gain[0], layer=0, ts=512)
    xt = x.reshape(t, d)
    x1 = _matmul_ln(o.reshape(t, d), hgrn_w_out[0].astype(cdt), xt, ln_gain[0, 0], ln_bias[0, 0],
                    tm=512, tn=1024)
    x1b = x1.astype(cdt)
    x2 = _peer_layer(x1, x1b, peer_w_q[0].astype(cdt), peer_subkeys[0].astype(cdt),
                     peer_u[0].astype(cdt), peer_v[0].astype(cdt), ln_gain[0, 1], ln_bias[0, 1])
    x2b = x2.astype(cdt)

    nkv = KV_HEADS * HEAD_DIM
    kv = _matmul(x2b, kv_w.astype(cdt), cdt, tm=512, tn=1024)
    to_heads = lambda a: a.reshape(bsz, seq, KV_HEADS, HEAD_DIM).transpose(0, 2, 1, 3)
    k_sh = to_heads(kv[:, :nkv])
    v_sh = to_heads(kv[:, nkv:])

    q = _matmul(x2b, attn_w_q[0].astype(cdt), cdt, tm=512, tn=1024)
    att = _attention(q.reshape(bsz, seq, -1), k_sh, v_sh, attn_sinks[0].astype(F32))
    x3 = _matmul_ln(att.reshape(t, -1), attn_w_out[0].astype(cdt), x2, ln_gain[1, 0], ln_bias[1, 0],
                    tm=512, tn=1024)
    x3b = x3.astype(cdt)
    x4 = _peer_layer(x3, x3b, peer_w_q[1].astype(cdt), peer_subkeys[1].astype(cdt),
                     peer_u[1].astype(cdt), peer_v[1].astype(cdt), ln_gain[1, 1], ln_bias[1, 1])
    return x4.reshape(bsz, seq, d)
```

```python
import functools
import math

import jax
import jax.numpy as jnp
from jax import lax
from jax.experimental import pallas as pl
from jax.experimental.pallas import tpu as pltpu

F32 = jnp.float32
MXU_DTYPE = jnp.bfloat16

DEPTH = 2
ALPHA = (2.0 * DEPTH) ** 0.25
LN_EPS = 1e-5
RMS_EPS = 1e-6

HGRN_DK = 128
HGRN_CHUNK = 32

HEAD_DIM = 64
KV_HEADS = 8
ATTN_BLOCK = 128
ATTN_SCALE = HEAD_DIM ** -0.5
NEG_INF = -1e30

PEER_HEADS = 8
PEER_NKEYS = 128
PEER_TOPK = 16
PEER_HALF = 128
GATE_ROWS = 32

LANES = 128
SUBLANES = 8
VMEM_LIMIT = 56 * 1024 * 1024


def _params(sem, flags=None):
    return pltpu.CompilerParams(dimension_semantics=sem, vmem_limit_bytes=VMEM_LIMIT, flags=flags)


def _sigmoid(x):
    return 1.0 / (1.0 + jnp.exp(-x))


def _dot(a, b):
    return jnp.dot(a, b, preferred_element_type=F32)


def _dot_nt(a, b):
    return lax.dot_general(a, b, (((1,), (1,)), ((), ())), preferred_element_type=F32)


def _dot_tn(a, b):
    return lax.dot_general(a, b, (((0,), (0,)), ((), ())), preferred_element_type=F32)


def _mm_kernel(a_ref, w_ref, o_ref):
    a = a_ref[...].astype(MXU_DTYPE)
    o_ref[...] = _dot(a, w_ref[...]).astype(o_ref.dtype)


def _matmul(a, w, out_dtype, tm, tn):
    m, k = a.shape
    n = w.shape[1]
    tm, tn = min(tm, m), min(tn, n)
    return pl.pallas_call(
        _mm_kernel,
        grid=(m // tm, n // tn),
        in_specs=[pl.BlockSpec((tm, k), lambda i, j: (i, 0)),
                  pl.BlockSpec((k, tn), lambda i, j: (0, j))],
        out_specs=pl.BlockSpec((tm, tn), lambda i, j: (i, j)),
        out_shape=jax.ShapeDtypeStruct((m, n), out_dtype),
        compiler_params=_params(("parallel", "arbitrary")),
        name="matmul",
    )(a, w)


def _ln_rows(z, g, b):
    mu = jnp.mean(z, axis=-1, keepdims=True)
    zc = z - mu
    var = jnp.mean(zc * zc, axis=-1, keepdims=True)
    return zc * lax.rsqrt(var + LN_EPS) * g + b


def _mm_ln_kernel(a_ref, w_ref, res_ref, g_ref, b_ref, o_ref, *, tn, nj, rows):
    j = pl.program_id(1)
    y = _dot(a_ref[...].astype(MXU_DTYPE), w_ref[...]) + ALPHA * res_ref[...]
    for jj in range(nj):
        @pl.when(j == jj)
        def _(jj=jj):
            o_ref[:, jj * tn:(jj + 1) * tn] = y

    @pl.when(j == nj - 1)
    def _():
        g = g_ref[...]
        b = b_ref[...]

        def body(r, carry):
            sl = pl.ds(pl.multiple_of(r * rows, rows), rows)
            o_ref[sl, :] = _ln_rows(o_ref[sl, :], g, b)
            return carry

        lax.fori_loop(0, o_ref.shape[0] // rows, body, 0)


def _matmul_ln(a, w, res, gain, bias, tm, tn):
    m, k = a.shape
    n = w.shape[1]
    tm, tn = min(tm, m), min(tn, n)
    nj = n // tn
    return pl.pallas_call(
        functools.partial(_mm_ln_kernel, tn=tn, nj=nj, rows=2 * SUBLANES),
        grid=(m // tm, nj),
        in_specs=[pl.BlockSpec((tm, k), lambda i, j: (i, 0)),
                  pl.BlockSpec((k, tn), lambda i, j: (0, j)),
                  pl.BlockSpec((tm, tn), lambda i, j: (i, j)),
                  pl.BlockSpec((1, n), lambda i, j: (0, 0)),
                  pl.BlockSpec((1, n), lambda i, j: (0, 0))],
        out_specs=pl.BlockSpec((tm, n), lambda i, j: (i, 0)),
        out_shape=jax.ShapeDtypeStruct((m, n), F32),
        compiler_params=_params(("parallel", "arbitrary")),
        name="matmul_ln",
    )(a, w, res, gain.reshape(1, n), bias.reshape(1, n))


def _res_ln_kernel(x_ref, y_ref, g_ref, b_ref, o_ref):
    o_ref[...] = _ln_rows(ALPHA * x_ref[...] + y_ref[...], g_ref[...], b_ref[...])


def _res_ln(x, y, gain, bias, tm):
    m, n = x.shape
    tm = min(tm, m)
    row = pl.BlockSpec((tm, n), lambda i: (i, 0))
    vec = pl.BlockSpec((1, n), lambda i: (0, 0))
    return pl.pallas_call(
        _res_ln_kernel,
        grid=(m // tm,),
        in_specs=[row, row, vec, vec],
        out_specs=row,
        out_shape=jax.ShapeDtypeStruct((m, n), F32),
        compiler_params=_params(("parallel",)),
        name="res_ln",
    )(x, y, gain.reshape(1, n), bias.reshape(1, n))


def _hgrn_kernel(x_ref, w_ref, lbl_ref, gain_ref, o_ref,
                 state_ref, qg_ref, kg_ref, k_ref, b_ref, v_ref, g_ref, *, layer, ts):
    sblk = pl.program_id(1)
    h = pl.program_id(2)
    dk = HGRN_DK
    c = HGRN_CHUNK

    @pl.when(sblk == 0)
    def _():
        state_ref[h] = jnp.zeros((dk, dk), F32)

    lg = lbl_ref[0]
    ex = jnp.exp(lg - jnp.max(lg, axis=0, keepdims=True))
    lb = jnp.sum(ex[:layer + 1], axis=0, keepdims=True) / jnp.sum(ex, axis=0, keepdims=True)

    proj = _dot(x_ref[0], w_ref[0])
    q_raw = proj[:, 0 * dk:1 * dk]
    f_raw = proj[:, 1 * dk:2 * dk]
    f = lb + (1.0 - lb) * _sigmoid(f_raw)
    k = 1.0 - f
    b = jnp.log(f)
    row = lax.broadcasted_iota(jnp.int32, (ts, dk), 0) % c
    sh = 1
    while sh < c:
        b = b + jnp.where(row >= sh, pltpu.roll(b, sh, axis=0), 0.0)
        sh *= 2
    qg_ref[...] = q_raw * _sigmoid(q_raw) * jnp.exp(b)
    kg_ref[...] = k * jnp.exp(-b)
    k_ref[...] = k
    b_ref[...] = b
    v_ref[...] = proj[:, 2 * dk:3 * dk]
    g_ref[...] = proj[:, 3 * dk:4 * dk]

    causal = (lax.broadcasted_iota(jnp.int32, (c, c), 0)
              >= lax.broadcasted_iota(jnp.int32, (c, c), 1))
    gain = gain_ref[...]

    def chunk(ci, st):
        sl = slice(ci * c, (ci + 1) * c)
        qg = qg_ref[sl, :].astype(MXU_DTYPE)
        vv = v_ref[sl, :].astype(MXU_DTYPE)
        bc = b_ref[sl, :]
        bl = bc[c - 1:c, :]
        a = jnp.where(causal, _dot_nt(qg, kg_ref[sl, :].astype(MXU_DTYPE)), 0.0)
        o = _dot(a.astype(MXU_DTYPE), vv) + _dot_nt(qg, st.astype(MXU_DTYPE))
        kd = (k_ref[sl, :] * jnp.exp(bl - bc)).astype(MXU_DTYPE)
        st = st * jnp.exp(bl) + _dot_tn(vv, kd)
        o = o * lax.rsqrt(jnp.mean(o * o, axis=-1, keepdims=True) + RMS_EPS) * gain
        gr = g_ref[sl, :]
        o_ref[0, sl, :] = (o * (gr * _sigmoid(gr))).astype(o_ref.dtype)
        return st

    st = state_ref[h]
    for ci in range(ts // c):
        st = chunk(ci, st)
    state_ref[h] = st


def _hgrn(xb, w_heads, lb_logits, gain, layer, ts):
    bsz, seq, d = xb.shape
    nh = w_heads.shape[0]
    dk = HGRN_DK
    ts = min(ts, seq)
    slots = lb_logits.shape[0]
    lbl = lb_logits.reshape(slots, nh, dk).transpose(1, 0, 2)
    buf = pltpu.VMEM((ts, dk), F32)
    return pl.pallas_call(
        functools.partial(_hgrn_kernel, layer=layer, ts=ts),
        grid=(bsz, seq // ts, nh),
        in_specs=[pl.BlockSpec((1, ts, d), lambda b, s, h: (b, s, 0)),
                  pl.BlockSpec((1, d, 4 * dk), lambda b, s, h: (h, 0, 0)),
                  pl.BlockSpec((1, slots, dk), lambda b, s, h: (h, 0, 0)),
                  pl.BlockSpec((1, dk), lambda b, s, h: (0, 0))],
        out_specs=pl.BlockSpec((1, ts, dk), lambda b, s, h: (b, s, h)),
        out_shape=jax.ShapeDtypeStruct((bsz, seq, d), MXU_DTYPE),
        scratch_shapes=[pltpu.VMEM((nh, dk, dk), F32), buf, buf, buf, buf, buf, buf],
        compiler_params=_params(("parallel", "arbitrary", "arbitrary")),
        name="hgrn2",
    )(xb, w_heads, lbl, gain.reshape(1, dk))


def _attn_kernel(q_ref, kp_ref, kc_ref, vp_ref, vc_ref, sink_ref, slope_ref, o_ref, *, group):
    n = pl.program_id(1)
    kvh = pl.program_id(2)
    blk = ATTN_BLOCK
    kk = jnp.concatenate([kp_ref[0, 0], kc_ref[0, 0]], axis=0)
    vv = jnp.concatenate([vp_ref[0, 0], vc_ref[0, 0]], axis=0)
    qi = lax.broadcasted_iota(jnp.int32, (blk, 2 * blk), 0)
    si = lax.broadcasted_iota(jnp.int32, (blk, 2 * blk), 1)
    dist = qi + blk - si
    mask = (dist >= 0) & (dist < blk) & ((si >= blk) | (n > 0))
    distf = dist.astype(F32)
    outs = []
    for g in range(group):
        slope = slope_ref[kvh * group + g]
        qh = q_ref[0, :, g * HEAD_DIM:(g + 1) * HEAD_DIM]
        s = _dot_nt(qh, kk) * ATTN_SCALE - slope * distf
        s = jnp.where(mask, s, NEG_INF)
        sink = sink_ref[kvh * group + g]
        m = jnp.maximum(jnp.max(s, axis=-1, keepdims=True), sink)
        p = jnp.exp(s - m)
        w = p / (jnp.sum(p, axis=-1, keepdims=True) + jnp.exp(sink - m))
        outs.append(_dot(w.astype(MXU_DTYPE), vv))
    o_ref[0] = jnp.concatenate(outs, axis=-1).astype(o_ref.dtype)


def _attention(q, k_sh, v_sh, sinks):
    bsz, seq, dq = q.shape
    group = dq // (KV_HEADS * HEAD_DIM)
    blk = ATTN_BLOCK
    gw = group * HEAD_DIM
    cur = pl.BlockSpec((1, 1, blk, HEAD_DIM), lambda b, n, k: (b, k, n, 0))
    prev = pl.BlockSpec((1, 1, blk, HEAD_DIM), lambda b, n, k: (b, k, jnp.maximum(n - 1, 0), 0))
    nheads = KV_HEADS * group
    slopes = jnp.asarray([2.0 ** (-8.0 * h / nheads) for h in range(1, nheads + 1)], F32)
    return pl.pallas_call(
        functools.partial(_attn_kernel, group=group),
        grid=(bsz, seq // blk, KV_HEADS),
        in_specs=[pl.BlockSpec((1, blk, gw), lambda b, n, k: (b, n, k)),
                  prev, cur, prev, cur,
                  pl.BlockSpec(memory_space=pltpu.SMEM),
                  pl.BlockSpec(memory_space=pltpu.SMEM)],
        out_specs=pl.BlockSpec((1, blk, gw), lambda b, n, k: (b, n, k)),
        out_shape=jax.ShapeDtypeStruct((bsz, seq, dq), MXU_DTYPE),
        compiler_params=_params(("parallel", "parallel", "arbitrary")),
        name="swa_attention",
    )(q, k_sh, k_sh, v_sh, v_sh, sinks, slopes)


def _top16(s):
    rid = lax.broadcasted_iota(jnp.int32, (PEER_TOPK, s.shape[1]), 0)
    tops = jnp.zeros((PEER_TOPK, s.shape[1]), F32)
    work = s
    for kth in range(PEER_TOPK):
        m = jnp.max(work, axis=0, keepdims=True)
        tops = jnp.where(rid == kth, m, tops)
        work = jnp.where(work == m, -jnp.inf, work)
    return tops


def _peer_sel_kernel(x_ref, wq_ref, sub_ref, s1_ref, s2_ref, st_ref, *, tb):
    q = _dot(x_ref[...].astype(MXU_DTYPE), wq_ref[...]).astype(MXU_DTYPE)
    s1 = _dot_nt(sub_ref[0, 0], q[:, :PEER_HALF])
    s2 = _dot_nt(sub_ref[0, 1], q[:, PEER_HALF:])
    for lb in range(tb // LANES):
        a1 = s1[:, lb * LANES:(lb + 1) * LANES]
        a2 = s2[:, lb * LANES:(lb + 1) * LANES]
        s1_ref[lb, 0] = a1
        s2_ref[lb, 0] = a2
        t1 = _top16(a1)
        t2 = _top16(a2)
        cands = [t1[0:1] + t2]
        for a in range(1, SUBLANES):
            cands.append(t1[a:a + 1] + t2[0:SUBLANES])
        cands.append(t1[SUBLANES:] + t2[0:1])
        cand = jnp.concatenate(cands, axis=0)
        best = _top16(cand)
        top = best[0:1]
        tau = best[PEER_TOPK - 1:PEER_TOPK]
        z = jnp.sum(jnp.where(cand >= tau, jnp.exp(cand - top), 0.0), axis=0, keepdims=True)
        zero = jnp.zeros_like(z)
        st_ref[lb, 0] = jnp.concatenate(
            [tau, t1[0:1], t2[0:1], 1.0 / z, zero, zero, zero, zero], axis=0)


def _peer_select(x, w_q, subkeys, tb):
    t, d = x.shape
    tb = min(tb, t)
    nlb = tb // LANES
    qd = 2 * PEER_HALF
    blk = lambda rows: pl.BlockSpec((nlb, 1, rows, LANES), lambda i, h: (i, h, 0, 0))
    shp = lambda rows: jax.ShapeDtypeStruct((t // LANES, PEER_HEADS, rows, LANES), F32)
    return pl.pallas_call(
        functools.partial(_peer_sel_kernel, tb=tb),
        grid=(t // tb, PEER_HEADS),
        in_specs=[pl.BlockSpec((tb, d), lambda i, h: (i, 0)),
                  pl.BlockSpec((d, qd), lambda i, h: (0, h)),
                  pl.BlockSpec((1, 2, PEER_NKEYS, PEER_HALF), lambda i, h: (h, 0, 0, 0))],
        out_specs=[blk(PEER_NKEYS), blk(PEER_NKEYS), blk(SUBLANES)],
        out_shape=[shp(PEER_NKEYS), shp(PEER_NKEYS), shp(SUBLANES)],
        compiler_params=_params(("parallel", "arbitrary")),
        name="peer_select",
    )(x, w_q, subkeys)


def _gelu(x):
    return 0.5 * x * (1.0 + lax.erf(x * (1.0 / math.sqrt(2.0))))


def _peer_dense_kernel(x_ref, u_ref, v_ref, s1_ref, s2_ref, st_ref, o_ref,
                       e2_ref, w1_ref, g_ref, at_ref, *, tb, ec):
    e = pl.program_id(1)
    nlb = tb // LANES
    nsub = ec // PEER_NKEYS

    @pl.when(e == 0)
    def _():
        o_ref[...] = jnp.zeros(o_ref.shape, F32)
        for lb in range(nlb):
            for h in range(PEER_HEADS):
                e2_ref[lb, h] = jnp.exp(s2_ref[lb, h] - st_ref[lb, h, 2:3, :])
                w1_ref[lb, h] = jnp.exp(s1_ref[lb, h] - st_ref[lb, h, 1:2, :]) * st_ref[lb, h, 3:4, :]

    for ii in range(nsub):
        i = e * nsub + ii
        for lb in range(nlb):
            lanes = slice(lb * LANES, (lb + 1) * LANES)
            for kt in range(PEER_NKEYS // GATE_ROWS):
                keys = slice(kt * GATE_ROWS, (kt + 1) * GATE_ROWS)
                gate = jnp.zeros((GATE_ROWS, LANES), F32)
                for h in range(PEER_HEADS):
                    tau = st_ref[lb, h, 0:1, :]
                    cs = s1_ref[lb, h, pl.ds(i, 1), :] + s2_ref[lb, h, keys, :]
                    val = e2_ref[lb, h, keys, :] * w1_ref[lb, h, pl.ds(i, 1), :]
                    gate = gate + jnp.where(cs >= tau, val, 0.0)
                g_ref[ii * PEER_NKEYS + kt * GATE_ROWS:ii * PEER_NKEYS + (kt + 1) * GATE_ROWS,
                      lanes] = gate

    ht = _dot_nt(u_ref[...], x_ref[...])
    at_ref[...] = (_gelu(ht) * g_ref[...]).astype(at_ref.dtype)
    o_ref[...] += _dot_tn(at_ref[...], v_ref[...])


def _peer_dense(x, u, v, s1, s2, st, tb, ec):
    t, d = x.shape
    tb = min(tb, t)
    nlb = tb // LANES
    ne = u.shape[0] // ec
    once = pl.Buffered(1)
    sblk = lambda rows: pl.BlockSpec((nlb, PEER_HEADS, rows, LANES), lambda i, e: (i, 0, 0, 0),
                                     pipeline_mode=once)
    return pl.pallas_call(
        functools.partial(_peer_dense_kernel, tb=tb, ec=ec),
        grid=(t // tb, ne),
        in_specs=[pl.BlockSpec((tb, d), lambda i, e: (i, 0), pipeline_mode=once),
                  pl.BlockSpec((ec, d), lambda i, e: (e, 0)),
                  pl.BlockSpec((ec, d), lambda i, e: (e, 0)),
                  sblk(PEER_NKEYS), sblk(PEER_NKEYS), sblk(SUBLANES)],
        out_specs=pl.BlockSpec((tb, d), lambda i, e: (i, 0)),
        out_shape=jax.ShapeDtypeStruct((t, d), F32),
        scratch_shapes=[pltpu.VMEM((nlb, PEER_HEADS, PEER_NKEYS, LANES), F32),
                        pltpu.VMEM((nlb, PEER_HEADS, PEER_NKEYS, LANES), F32),
                        pltpu.VMEM((ec, tb), F32),
                        pltpu.VMEM((ec, tb), MXU_DTYPE)],
        compiler_params=_params(("parallel", "arbitrary")),
        name="peer_dense",
    )(x, u, v, s1, s2, st)


def _peer_layer(x, xb, w_q, subkeys, u, v, gain, bias):
    s1, s2, st = _peer_select(xb, w_q, subkeys, tb=512)
    y = _peer_dense(xb, u, v, s1, s2, st, tb=512, ec=512)
    return _res_ln(x, y, gain, bias, tm=256)


def kernel(x, hgrn_w_in, hgrn_lb_logits, hgrn_norm_gain, hgrn_w_out, kv_w, attn_w_q, attn_sinks,
           attn_w_out, peer_w_q, peer_subkeys, peer_u, peer_v, ln_gain, ln_bias):
    bsz, seq, d = x.shape
    t = bsz * seq
    cdt = MXU_DTYPE
    dk = HGRN_DK
    nh = d // dk

    w_in = hgrn_w_in[0].astype(cdt).reshape(d, 4, nh, dk).transpose(2, 0, 1, 3).reshape(nh, d, 4 * dk)
    o = _hgrn(x.astype(cdt), w_in, hgrn_lb_logits, hgrn_norm_gain[0], layer=0, ts=512)
    xt = x.reshape(t, d)
    x1 = _matmul_ln(o.reshape(t, d), hgrn_w_out[0].astype(cdt), xt, ln_gain[0, 0], ln_bias[0, 0],
                    tm=512, tn=1024)
    x1b = x1.astype(cdt)
    x2 = _peer_layer(x1, x1b, peer_w_q[0].astype(cdt), peer_subkeys[0].astype(cdt),
                     peer_u[0].astype(cdt), peer_v[0].astype(cdt), ln_gain[0, 1], ln_bias[0, 1])
    x2b = x2.astype(cdt)

    nkv = KV_HEADS * HEAD_DIM
    kv = _matmul(x2b, kv_w.astype(cdt), cdt, tm=512, tn=1024)
    to_heads = lambda a: a.reshape(bsz, seq, KV_HEADS, HEAD_DIM).transpose(0, 2, 1, 3)
    k_sh = to_heads(kv[:, :nkv])
    v_sh = to_heads(kv[:, nkv:])

    q = _matmul(x2b, attn_w_q[0].astype(cdt), cdt, tm=512, tn=1024)
    att = _attention(q.reshape(bsz, seq, -1), k_sh, v_sh, attn_sinks[0].astype(F32))
    x3 = _matmul_ln(att.reshape(t, -1), attn_w_out[0].astype(cdt), x2, ln_gain[1, 0], ln_bias[1, 0],
                    tm=512, tn=1024)
    x3b = x3.astype(cdt)
    x4 = _peer_layer(x3, x3b, peer_w_q[1].astype(cdt), peer_subkeys[1].astype(cdt),
                     peer_u[1].astype(cdt), peer_v[1].astype(cdt), ln_gain[1, 1], ln_bias[1, 1])
    return x4.reshape(bsz, seq, d)
```

```python
import functools
import math

import jax
import jax.numpy as jnp
from jax import lax
from jax.experimental import pallas as pl
from jax.experimental.pallas import tpu as pltpu

F32 = jnp.float32
MXU_DTYPE = jnp.bfloat16

DEPTH = 2
ALPHA = (2.0 * DEPTH) ** 0.25
LN_EPS = 1e-5
RMS_EPS = 1e-6

HGRN_DK = 128
HGRN_CHUNK = 32

HEAD_DIM = 64
KV_HEADS = 8
ATTN_BLOCK = 128
ATTN_SCALE = HEAD_DIM ** -0.5
NEG_INF = -1e30

PEER_HEADS = 8
PEER_NKEYS = 128
PEER_TOPK = 16
PEER_HALF = 128
GATE_ROWS = 32

LANES = 128
SUBLANES = 8
VMEM_LIMIT = 56 * 1024 * 1024


def _params(sem, flags=None):
    return pltpu.CompilerParams(dimension_semantics=sem, vmem_limit_bytes=VMEM_LIMIT, flags=flags)


def _sigmoid(x):
    return 1.0 / (1.0 + jnp.exp(-x))


def _dot(a, b):
    return jnp.dot(a, b, preferred_element_type=F32)


def _dot_nt(a, b):
    return lax.dot_general(a, b, (((1,), (1,)), ((), ())), preferred_element_type=F32)


def _dot_tn(a, b):
    return lax.dot_general(a, b, (((0,), (0,)), ((), ())), preferred_element_type=F32)


def _mm_kernel(a_ref, w_ref, o_ref):
    a = a_ref[...].astype(MXU_DTYPE)
    o_ref[...] = _dot(a, w_ref[...]).astype(o_ref.dtype)


def _matmul(a, w, out_dtype, tm, tn):
    m, k = a.shape
    n = w.shape[1]
    tm, tn = min(tm, m), min(tn, n)
    return pl.pallas_call(
        _mm_kernel,
        grid=(m // tm, n // tn),
        in_specs=[pl.BlockSpec((tm, k), lambda i, j: (i, 0)),
                  pl.BlockSpec((k, tn), lambda i, j: (0, j))],
        out_specs=pl.BlockSpec((tm, tn), lambda i, j: (i, j)),
        out_shape=jax.ShapeDtypeStruct((m, n), out_dtype),
        compiler_params=_params(("parallel", "arbitrary")),
        name="matmul",
    )(a, w)


def _ln_rows(z, g, b):
    mu = jnp.mean(z, axis=-1, keepdims=True)
    zc = z - mu
    var = jnp.mean(zc * zc, axis=-1, keepdims=True)
    return zc * lax.rsqrt(var + LN_EPS) * g + b


def _mm_ln_kernel(a_ref, w_ref, res_ref, g_ref, b_ref, o_ref, ob_ref, *, tn, nj, rows):
    j = pl.program_id(1)
    y = _dot(a_ref[...].astype(MXU_DTYPE), w_ref[...]) + ALPHA * res_ref[...]
    for jj in range(nj):
        @pl.when(j == jj)
        def _(jj=jj):
            o_ref[:, jj * tn:(jj + 1) * tn] = y

    @pl.when(j == nj - 1)
    def _():
        g = g_ref[...]
        b = b_ref[...]

        def body(r, carry):
            sl = pl.ds(pl.multiple_of(r * rows, rows), rows)
            z = _ln_rows(o_ref[sl, :], g, b)
            o_ref[sl, :] = z
            ob_ref[sl, :] = z.astype(ob_ref.dtype)
            return carry

        lax.fori_loop(0, o_ref.shape[0] // rows, body, 0)


def _matmul_ln(a, w, res, gain, bias, tm, tn):
    m, k = a.shape
    n = w.shape[1]
    tm, tn = min(tm, m), min(tn, n)
    nj = n // tn
    full = pl.BlockSpec((tm, n), lambda i, j: (i, 0))
    return pl.pallas_call(
        functools.partial(_mm_ln_kernel, tn=tn, nj=nj, rows=2 * SUBLANES),
        grid=(m // tm, nj),
        in_specs=[pl.BlockSpec((tm, k), lambda i, j: (i, 0)),
                  pl.BlockSpec((k, tn), lambda i, j: (0, j)),
                  pl.BlockSpec((tm, tn), lambda i, j: (i, j)),
                  pl.BlockSpec((1, n), lambda i, j: (0, 0)),
                  pl.BlockSpec((1, n), lambda i, j: (0, 0))],
        out_specs=[full, full],
        out_shape=[jax.ShapeDtypeStruct((m, n), F32), jax.ShapeDtypeStruct((m, n), MXU_DTYPE)],
        compiler_params=_params(("parallel", "arbitrary")),
        name="matmul_ln",
    )(a, w, res, gain.reshape(1, n), bias.reshape(1, n))


def _res_ln_kernel(x_ref, y_ref, g_ref, b_ref, *o_refs):
    z = _ln_rows(ALPHA * x_ref[...] + y_ref[...], g_ref[...], b_ref[...])
    for o_ref in o_refs:
        o_ref[...] = z.astype(o_ref.dtype)


def _res_ln(x, y, gain, bias, tm, with_copy):
    m, n = x.shape
    tm = min(tm, m)
    row = pl.BlockSpec((tm, n), lambda i: (i, 0))
    vec = pl.BlockSpec((1, n), lambda i: (0, 0))
    dtypes = (F32, MXU_DTYPE) if with_copy else (F32,)
    return pl.pallas_call(
        _res_ln_kernel,
        grid=(m // tm,),
        in_specs=[row, row, vec, vec],
        out_specs=[row] * len(dtypes),
        out_shape=[jax.ShapeDtypeStruct((m, n), dt) for dt in dtypes],
        compiler_params=_params(("parallel",)),
        name="res_ln",
    )(x, y, gain.reshape(1, n), bias.reshape(1, n))


def _hgrn_kernel(x_ref, w_ref, lbl_ref, gain_ref, o_ref,
                 state_ref, qg_ref, kg_ref, k_ref, b_ref, v_ref, g_ref, *, layer, ts):
    sblk = pl.program_id(1)
    h = pl.program_id(2)
    dk = HGRN_DK
    c = HGRN_CHUNK

    @pl.when(sblk == 0)
    def _():
        state_ref[h] = jnp.zeros((dk, dk), F32)

    lg = lbl_ref[0]
    ex = jnp.exp(lg - jnp.max(lg, axis=0, keepdims=True))
    lb = jnp.sum(ex[:layer + 1], axis=0, keepdims=True) / jnp.sum(ex, axis=0, keepdims=True)

    proj = _dot(x_ref[0], w_ref[0])
    q_raw = proj[:, 0 * dk:1 * dk]
    f_raw = proj[:, 1 * dk:2 * dk]
    f = lb + (1.0 - lb) * _sigmoid(f_raw)
    k = 1.0 - f
    b = jnp.log(f)
    row = lax.broadcasted_iota(jnp.int32, (ts, dk), 0) % c
    sh = 1
    while sh < c:
        b = b + jnp.where(row >= sh, pltpu.roll(b, sh, axis=0), 0.0)
        sh *= 2
    qg_ref[...] = q_raw * _sigmoid(q_raw) * jnp.exp(b)
    kg_ref[...] = k * jnp.exp(-b)
    k_ref[...] = k
    b_ref[...] = b
    v_ref[...] = proj[:, 2 * dk:3 * dk]
    g_ref[...] = proj[:, 3 * dk:4 * dk]

    causal = (lax.broadcasted_iota(jnp.int32, (c, c), 0)
              >= lax.broadcasted_iota(jnp.int32, (c, c), 1))
    gain = gain_ref[...]

    def chunk(ci, st):
        sl = slice(ci * c, (ci + 1) * c)
        qg = qg_ref[sl, :].astype(MXU_DTYPE)
        vv = v_ref[sl, :].astype(MXU_DTYPE)
        bc = b_ref[sl, :]
        bl = bc[c - 1:c, :]
        a = jnp.where(causal, _dot_nt(qg, kg_ref[sl, :].astype(MXU_DTYPE)), 0.0)
        o = _dot(a.astype(MXU_DTYPE), vv) + _dot_nt(qg, st.astype(MXU_DTYPE))
        kd = (k_ref[sl, :] * jnp.exp(bl - bc)).astype(MXU_DTYPE)
        st = st * jnp.exp(bl) + _dot_tn(vv, kd)
        o = o * lax.rsqrt(jnp.mean(o * o, axis=-1, keepdims=True) + RMS_EPS) * gain
        gr = g_ref[sl, :]
        o_ref[0, sl, :] = (o * (gr * _sigmoid(gr))).astype(o_ref.dtype)
        return st

    st = state_ref[h]
    for ci in range(ts // c):
        st = chunk(ci, st)
    state_ref[h] = st


def _hgrn(xb, w_heads, lb_logits, gain, layer, ts):
    bsz, seq, d = xb.shape
    nh = w_heads.shape[0]
    dk = HGRN_DK
    ts = min(ts, seq)
    slots = lb_logits.shape[0]
    lbl = lb_logits.reshape(slots, nh, dk).transpose(1, 0, 2)
    buf = pltpu.VMEM((ts, dk), F32)
    return pl.pallas_call(
        functools.partial(_hgrn_kernel, layer=layer, ts=ts),
        grid=(bsz, seq // ts, nh),
        in_specs=[pl.BlockSpec((1, ts, d), lambda b, s, h: (b, s, 0)),
                  pl.BlockSpec((1, d, 4 * dk), lambda b, s, h: (h, 0, 0)),
                  pl.BlockSpec((1, slots, dk), lambda b, s, h: (h, 0, 0)),
                  pl.BlockSpec((1, dk), lambda b, s, h: (0, 0))],
        out_specs=pl.BlockSpec((1, ts, dk), lambda b, s, h: (b, s, h)),
        out_shape=jax.ShapeDtypeStruct((bsz, seq, d), MXU_DTYPE),
        scratch_shapes=[pltpu.VMEM((nh, dk, dk), F32), buf, buf, buf, buf, buf, buf],
        compiler_params=_params(("parallel", "arbitrary", "arbitrary")),
        name="hgrn2",
    )(xb, w_heads, lbl, gain.reshape(1, dk))


def _attn_kernel(q_ref, kp_ref, kc_ref, vp_ref, vc_ref, bias_ref, sink_ref, o_ref, *, group):
    kvh = pl.program_id(0)
    n = pl.program_id(2)
    blk = ATTN_BLOCK
    kk = jnp.concatenate([kp_ref[0, 0], kc_ref[0, 0]], axis=0)
    vv = jnp.concatenate([vp_ref[0, 0], vc_ref[0, 0]], axis=0)
    si = lax.broadcasted_iota(jnp.int32, (1, 2 * blk), 1)
    first = jnp.where((si >= blk) | (n > 0), 0.0, NEG_INF)
    outs = []
    for g in range(group):
        qh = q_ref[0, :, g * HEAD_DIM:(g + 1) * HEAD_DIM] * ATTN_SCALE
        s = _dot_nt(qh, kk) + bias_ref[g] + first
        sink = sink_ref[kvh * group + g]
        m = jnp.maximum(jnp.max(s, axis=-1, keepdims=True), sink)
        p = jnp.exp(s - m)
        w = p / (jnp.sum(p, axis=-1, keepdims=True) + jnp.exp(sink - m))
        outs.append(_dot(w.astype(MXU_DTYPE), vv))
    o_ref[0] = jnp.concatenate(outs, axis=-1).astype(o_ref.dtype)


def _attention(q, k_sh, v_sh, sinks):
    bsz, seq, dq = q.shape
    group = dq // (KV_HEADS * HEAD_DIM)
    blk = ATTN_BLOCK
    gw = group * HEAD_DIM
    cur = pl.BlockSpec((1, 1, blk, HEAD_DIM), lambda k, b, n: (b, k, n, 0))
    prev = pl.BlockSpec((1, 1, blk, HEAD_DIM), lambda k, b, n: (b, k, jnp.maximum(n - 1, 0), 0))
    nheads = KV_HEADS * group
    slopes = jnp.asarray([2.0 ** (-8.0 * h / nheads) for h in range(1, nheads + 1)], F32)
    dist = (jnp.arange(blk)[:, None] + blk - jnp.arange(2 * blk)[None, :])
    bias = jnp.where((dist >= 0) & (dist < blk),
                     -(slopes[:, None, None] * dist.astype(F32)[None]), NEG_INF)
    return pl.pallas_call(
        functools.partial(_attn_kernel, group=group),
        grid=(KV_HEADS, bsz, seq // blk),
        in_specs=[pl.BlockSpec((1, blk, gw), lambda k, b, n: (b, n, k)),
                  prev, cur, prev, cur,
                  pl.BlockSpec((group, blk, 2 * blk), lambda k, b, n: (k, 0, 0)),
                  pl.BlockSpec(memory_space=pltpu.SMEM)],
        out_specs=pl.BlockSpec((1, blk, gw), lambda k, b, n: (b, n, k)),
        out_shape=jax.ShapeDtypeStruct((bsz, seq, dq), MXU_DTYPE),
        compiler_params=_params(("parallel", "parallel", "arbitrary")),
        name="swa_attention",
    )(q, k_sh, k_sh, v_sh, v_sh, bias, sinks)


def _top16(s):
    rid = lax.broadcasted_iota(jnp.int32, (PEER_TOPK, s.shape[1]), 0)
    tops = jnp.zeros((PEER_TOPK, s.shape[1]), F32)
    work = s
    for kth in range(PEER_TOPK):
        m = jnp.max(work, axis=0, keepdims=True)
        tops = jnp.where(rid == kth, m, tops)
        work = jnp.where(work == m, -jnp.inf, work)
    return tops


def _peer_sel_kernel(x_ref, wq_ref, sub_ref, s1_ref, s2_ref, st_ref, *, tb):
    q = _dot(x_ref[...].astype(MXU_DTYPE), wq_ref[...]).astype(MXU_DTYPE)
    s1 = _dot_nt(sub_ref[0, 0], q[:, :PEER_HALF])
    s2 = _dot_nt(sub_ref[0, 1], q[:, PEER_HALF:])
    for lb in range(tb // LANES):
        a1 = s1[:, lb * LANES:(lb + 1) * LANES]
        a2 = s2[:, lb * LANES:(lb + 1) * LANES]
        s1_ref[lb, 0] = a1
        s2_ref[lb, 0] = a2
        t1 = _top16(a1)
        t2 = _top16(a2)
        cands = [t1[0:1] + t2]
        for a in range(1, SUBLANES):
            cands.append(t1[a:a + 1] + t2[0:SUBLANES])
        cands.append(t1[SUBLANES:] + t2[0:1])
        cand = jnp.concatenate(cands, axis=0)
        best = _top16(cand)
        top = best[0:1]
        tau = best[PEER_TOPK - 1:PEER_TOPK]
        z = jnp.sum(jnp.where(cand >= tau, jnp.exp(cand - top), 0.0), axis=0, keepdims=True)
        zero = jnp.zeros_like(z)
        st_ref[lb, 0] = jnp.concatenate(
            [tau, t1[0:1], t2[0:1], 1.0 / z, zero, zero, zero, zero], axis=0)


def _peer_select(x, w_q, subkeys, tb):
    t, d = x.shape
    tb = min(tb, t)
    nlb = tb // LANES
    qd = 2 * PEER_HALF
    blk = lambda rows: pl.BlockSpec((nlb, 1, rows, LANES), lambda i, h: (i, h, 0, 0))
    shp = lambda rows: jax.ShapeDtypeStruct((t // LANES, PEER_HEADS, rows, LANES), F32)
    return pl.pallas_call(
        functools.partial(_peer_sel_kernel, tb=tb),
        grid=(t // tb, PEER_HEADS),
        in_specs=[pl.BlockSpec((tb, d), lambda i, h: (i, 0)),
                  pl.BlockSpec((d, qd), lambda i, h: (0, h)),
                  pl.BlockSpec((1, 2, PEER_NKEYS, PEER_HALF), lambda i, h: (h, 0, 0, 0))],
        out_specs=[blk(PEER_NKEYS), blk(PEER_NKEYS), blk(SUBLANES)],
        out_shape=[shp(PEER_NKEYS), shp(PEER_NKEYS), shp(SUBLANES)],
        compiler_params=_params(("parallel", "arbitrary")),
        name="peer_select",
    )(x, w_q, subkeys)


def _gelu(x):
    return 0.5 * x * (1.0 + lax.erf(x * (1.0 / math.sqrt(2.0))))


def _peer_dense_kernel(x_ref, u_ref, v_ref, s1_ref, s2_ref, st_ref, o_ref,
                       e2_ref, w1_ref, g_ref, at_ref, *, tb, ec):
    e = pl.program_id(1)
    nlb = tb // LANES
    nsub = ec // PEER_NKEYS

    @pl.when(e == 0)
    def _():
        o_ref[...] = jnp.zeros(o_ref.shape, F32)
        for lb in range(nlb):
            for h in range(PEER_HEADS):
                e2_ref[lb, h] = jnp.exp(s2_ref[lb, h] - st_ref[lb, h, 2:3, :])
                w1_ref[lb, h] = jnp.exp(s1_ref[lb, h] - st_ref[lb, h, 1:2, :]) * st_ref[lb, h, 3:4, :]

    for ii in range(nsub):
        i = e * nsub + ii
        for lb in range(nlb):
            lanes = slice(lb * LANES, (lb + 1) * LANES)
            for kt in range(PEER_NKEYS // GATE_ROWS):
                keys = slice(kt * GATE_ROWS, (kt + 1) * GATE_ROWS)
                gate = jnp.zeros((GATE_ROWS, LANES), F32)
                for h in range(PEER_HEADS):
                    tau = st_ref[lb, h, 0:1, :]
                    cs = s1_ref[lb, h, pl.ds(i, 1), :] + s2_ref[lb, h, keys, :]
                    val = e2_ref[lb, h, keys, :] * w1_ref[lb, h, pl.ds(i, 1), :]
                    gate = gate + jnp.where(cs >= tau, val, 0.0)
                g_ref[ii * PEER_NKEYS + kt * GATE_ROWS:ii * PEER_NKEYS + (kt + 1) * GATE_ROWS,
                      lanes] = gate

    ht = _dot_nt(u_ref[...], x_ref[...])
    at_ref[...] = (_gelu(ht) * g_ref[...]).astype(at_ref.dtype)
    o_ref[...] += _dot_tn(at_ref[...], v_ref[...])


def _peer_dense(x, u, v, s1, s2, st, tb, ec):
    t, d = x.shape
    tb = min(tb, t)
    nlb = tb // LANES
    ne = u.shape[0] // ec
    once = pl.Buffered(1)
    sblk = lambda rows: pl.BlockSpec((nlb, PEER_HEADS, rows, LANES), lambda i, e: (i, 0, 0, 0),
                                     pipeline_mode=once)
    return pl.pallas_call(
        functools.partial(_peer_dense_kernel, tb=tb, ec=ec),
        grid=(t // tb, ne),
        in_specs=[pl.BlockSpec((tb, d), lambda i, e: (i, 0), pipeline_mode=once),
                  pl.BlockSpec((ec, d), lambda i, e: (e, 0)),
                  pl.BlockSpec((ec, d), lambda i, e: (e, 0)),
                  sblk(PEER_NKEYS), sblk(PEER_NKEYS), sblk(SUBLANES)],
        out_specs=pl.BlockSpec((tb, d), lambda i, e: (i, 0)),
        out_shape=jax.ShapeDtypeStruct((t, d), F32),
        scratch_shapes=[pltpu.VMEM((nlb, PEER_HEADS, PEER_NKEYS, LANES), F32),
                        pltpu.VMEM((nlb, PEER_HEADS, PEER_NKEYS, LANES), F32),
                        pltpu.VMEM((ec, tb), F32),
                        pltpu.VMEM((ec, tb), MXU_DTYPE)],
        compiler_params=_params(("parallel", "arbitrary")),
        name="peer_dense",
    )(x, u, v, s1, s2, st)


def _peer_layer(x, xb, w_q, subkeys, u, v, gain, bias, with_copy):
    s1, s2, st = _peer_select(xb, w_q, subkeys, tb=512)
    y = _peer_dense(xb, u, v, s1, s2, st, tb=512, ec=512)
    return _res_ln(x, y, gain, bias, tm=256, with_copy=with_copy)


def kernel(x, hgrn_w_in, hgrn_lb_logits, hgrn_norm_gain, hgrn_w_out, kv_w, attn_w_q, attn_sinks,
           attn_w_out, peer_w_q, peer_subkeys, peer_u, peer_v, ln_gain, ln_bias):
    bsz, seq, d = x.shape
    t = bsz * seq
    cdt = MXU_DTYPE
    dk = HGRN_DK
    nh = d // dk

    w_in = hgrn_w_in[0].astype(cdt).reshape(d, 4, nh, dk).transpose(2, 0, 1, 3).reshape(nh, d, 4 * dk)
    o = _hgrn(x.astype(cdt), w_in, hgrn_lb_logits, hgrn_norm_gain[0], layer=0, ts=512)
    xt = x.reshape(t, d)
    x1, x1b = _matmul_ln(o.reshape(t, d), hgrn_w_out[0].astype(cdt), xt, ln_gain[0, 0], ln_bias[0, 0],
                         tm=512, tn=512)
    x2, x2b = _peer_layer(x1, x1b, peer_w_q[0].astype(cdt), peer_subkeys[0].astype(cdt),
                          peer_u[0].astype(cdt), peer_v[0].astype(cdt), ln_gain[0, 1], ln_bias[0, 1],
                          with_copy=True)

    nkv = KV_HEADS * HEAD_DIM
    kv = _matmul(x2b, kv_w.astype(cdt), cdt, tm=512, tn=1024)
    to_heads = lambda a: a.reshape(bsz, seq, KV_HEADS, HEAD_DIM).transpose(0, 2, 1, 3)
    k_sh = to_heads(kv[:, :nkv])
    v_sh = to_heads(kv[:, nkv:])

    q = _matmul(x2b, attn_w_q[0].astype(cdt), cdt, tm=512, tn=1024)
    att = _attention(q.reshape(bsz, seq, -1), k_sh, v_sh, attn_sinks[0].astype(F32))
    x3, x3b = _matmul_ln(att.reshape(t, -1), attn_w_out[0].astype(cdt), x2, ln_gain[1, 0], ln_bias[1, 0],
                         tm=512, tn=512)
    (x4,) = _peer_layer(x3, x3b, peer_w_q[1].astype(cdt), peer_subkeys[1].astype(cdt),
                        peer_u[1].astype(cdt), peer_v[1].astype(cdt), ln_gain[1, 1], ln_bias[1, 1],
                        with_copy=False)
    return x4.reshape(bsz, seq, d)
```

```python
import functools
import math

import jax
import jax.numpy as jnp
from jax import lax
from jax.experimental import pallas as pl
from jax.experimental.pallas import tpu as pltpu

F32 = jnp.float32
MXU_DTYPE = jnp.bfloat16

DEPTH = 2
ALPHA = (2.0 * DEPTH) ** 0.25
LN_EPS = 1e-5
RMS_EPS = 1e-6

HGRN_DK = 128
HGRN_CHUNK = 32

HEAD_DIM = 64
KV_HEADS = 8
ATTN_BLOCK = 128
ATTN_SCALE = HEAD_DIM ** -0.5
NEG_INF = -1e30

PEER_HEADS = 8
PEER_NKEYS = 128
PEER_TOPK = 16
PEER_HALF = 128
GATE_ROWS = 32
PEER_HIDDEN_DTYPE = jnp.float8_e4m3fn
FP8_ROW_TOP = 240.0

LANES = 128
SUBLANES = 8
VMEM_LIMIT = 56 * 1024 * 1024


def _params(sem, flags=None):
    return pltpu.CompilerParams(dimension_semantics=sem, vmem_limit_bytes=VMEM_LIMIT, flags=flags)


def _sigmoid(x):
    return 1.0 / (1.0 + jnp.exp(-x))


def _dot(a, b):
    return jnp.dot(a, b, preferred_element_type=F32)


def _dot_nt(a, b):
    return lax.dot_general(a, b, (((1,), (1,)), ((), ())), preferred_element_type=F32)


def _dot_tn(a, b):
    return lax.dot_general(a, b, (((0,), (0,)), ((), ())), preferred_element_type=F32)


def _mm_kernel(a_ref, w_ref, o_ref):
    a = a_ref[...].astype(MXU_DTYPE)
    o_ref[...] = _dot(a, w_ref[...]).astype(o_ref.dtype)


def _matmul(a, w, out_dtype, tm, tn):
    m, k = a.shape
    n = w.shape[1]
    tm, tn = min(tm, m), min(tn, n)
    return pl.pallas_call(
        _mm_kernel,
        grid=(m // tm, n // tn),
        in_specs=[pl.BlockSpec((tm, k), lambda i, j: (i, 0)),
                  pl.BlockSpec((k, tn), lambda i, j: (0, j))],
        out_specs=pl.BlockSpec((tm, tn), lambda i, j: (i, j)),
        out_shape=jax.ShapeDtypeStruct((m, n), out_dtype),
        compiler_params=_params(("parallel", "arbitrary")),
        name="matmul",
    )(a, w)


def _ln_rows(z, g, b):
    mu = jnp.mean(z, axis=-1, keepdims=True)
    zc = z - mu
    var = jnp.mean(zc * zc, axis=-1, keepdims=True)
    return zc * lax.rsqrt(var + LN_EPS) * g + b


def _mm_ln_kernel(a_ref, w_ref, res_ref, g_ref, b_ref, o_ref, ob_ref, *, tn, nj, rows):
    j = pl.program_id(1)
    y = _dot(a_ref[...].astype(MXU_DTYPE), w_ref[...]) + ALPHA * res_ref[...]
    for jj in range(nj):
        @pl.when(j == jj)
        def _(jj=jj):
            o_ref[:, jj * tn:(jj + 1) * tn] = y

    @pl.when(j == nj - 1)
    def _():
        g = g_ref[...]
        b = b_ref[...]

        def body(r, carry):
            sl = pl.ds(pl.multiple_of(r * rows, rows), rows)
            z = _ln_rows(o_ref[sl, :], g, b)
            o_ref[sl, :] = z
            ob_ref[sl, :] = z.astype(ob_ref.dtype)
            return carry

        lax.fori_loop(0, o_ref.shape[0] // rows, body, 0)


def _matmul_ln(a, w, res, gain, bias, tm, tn):
    m, k = a.shape
    n = w.shape[1]
    tm, tn = min(tm, m), min(tn, n)
    nj = n // tn
    full = pl.BlockSpec((tm, n), lambda i, j: (i, 0))
    return pl.pallas_call(
        functools.partial(_mm_ln_kernel, tn=tn, nj=nj, rows=2 * SUBLANES),
        grid=(m // tm, nj),
        in_specs=[pl.BlockSpec((tm, k), lambda i, j: (i, 0)),
                  pl.BlockSpec((k, tn), lambda i, j: (0, j)),
                  pl.BlockSpec((tm, tn), lambda i, j: (i, j)),
                  pl.BlockSpec((1, n), lambda i, j: (0, 0)),
                  pl.BlockSpec((1, n), lambda i, j: (0, 0))],
        out_specs=[full, full],
        out_shape=[jax.ShapeDtypeStruct((m, n), F32), jax.ShapeDtypeStruct((m, n), MXU_DTYPE)],
        compiler_params=_params(("parallel", "arbitrary")),
        name="matmul_ln",
    )(a, w, res, gain.reshape(1, n), bias.reshape(1, n))


def _res_ln_kernel(x_ref, y_ref, g_ref, b_ref, *o_refs):
    z = _ln_rows(ALPHA * x_ref[...] + y_ref[...], g_ref[...], b_ref[...])
    for o_ref in o_refs:
        o_ref[...] = z.astype(o_ref.dtype)


def _res_ln(x, y, gain, bias, tm, with_copy):
    m, n = x.shape
    tm = min(tm, m)
    row = pl.BlockSpec((tm, n), lambda i: (i, 0))
    vec = pl.BlockSpec((1, n), lambda i: (0, 0))
    dtypes = (F32, MXU_DTYPE) if with_copy else (F32,)
    return pl.pallas_call(
        _res_ln_kernel,
        grid=(m // tm,),
        in_specs=[row, row, vec, vec],
        out_specs=[row] * len(dtypes),
        out_shape=[jax.ShapeDtypeStruct((m, n), dt) for dt in dtypes],
        compiler_params=_params(("parallel",)),
        name="res_ln",
    )(x, y, gain.reshape(1, n), bias.reshape(1, n))


def _hgrn_kernel(x_ref, w_ref, lbl_ref, gain_ref, o_ref,
                 state_ref, qg_ref, kg_ref, k_ref, b_ref, v_ref, g_ref, *, layer, ts):
    sblk = pl.program_id(1)
    h = pl.program_id(2)
    dk = HGRN_DK
    c = HGRN_CHUNK

    @pl.when(sblk == 0)
    def _():
        state_ref[h] = jnp.zeros((dk, dk), F32)

    lg = lbl_ref[0]
    ex = jnp.exp(lg - jnp.max(lg, axis=0, keepdims=True))
    lb = jnp.sum(ex[:layer + 1], axis=0, keepdims=True) / jnp.sum(ex, axis=0, keepdims=True)

    proj = _dot(x_ref[0], w_ref[0])
    q_raw = proj[:, 0 * dk:1 * dk]
    f_raw = proj[:, 1 * dk:2 * dk]
    f = lb + (1.0 - lb) * _sigmoid(f_raw)
    k = 1.0 - f
    b = jnp.log(f)
    row = lax.broadcasted_iota(jnp.int32, (ts, dk), 0) % c
    sh = 1
    while sh < c:
        b = b + jnp.where(row >= sh, pltpu.roll(b, sh, axis=0), 0.0)
        sh *= 2
    qg_ref[...] = q_raw * _sigmoid(q_raw) * jnp.exp(b)
    kg_ref[...] = k * jnp.exp(-b)
    k_ref[...] = k
    b_ref[...] = b
    v_ref[...] = proj[:, 2 * dk:3 * dk]
    g_ref[...] = proj[:, 3 * dk:4 * dk]

    causal = (lax.broadcasted_iota(jnp.int32, (c, c), 0)
              >= lax.broadcasted_iota(jnp.int32, (c, c), 1))
    gain = gain_ref[...]

    def chunk(ci, st):
        sl = slice(ci * c, (ci + 1) * c)
        qg = qg_ref[sl, :].astype(MXU_DTYPE)
        vv = v_ref[sl, :].astype(MXU_DTYPE)
        bc = b_ref[sl, :]
        bl = bc[c - 1:c, :]
        a = jnp.where(causal, _dot_nt(qg, kg_ref[sl, :].astype(MXU_DTYPE)), 0.0)
        o = _dot(a.astype(MXU_DTYPE), vv) + _dot_nt(qg, st.astype(MXU_DTYPE))
        kd = (k_ref[sl, :] * jnp.exp(bl - bc)).astype(MXU_DTYPE)
        st = st * jnp.exp(bl) + _dot_tn(vv, kd)
        o = o * lax.rsqrt(jnp.mean(o * o, axis=-1, keepdims=True) + RMS_EPS) * gain
        gr = g_ref[sl, :]
        o_ref[0, sl, :] = (o * (gr * _sigmoid(gr))).astype(o_ref.dtype)
        return st

    st = state_ref[h]
    for ci in range(ts // c):
        st = chunk(ci, st)
    state_ref[h] = st


def _hgrn(xb, w_heads, lb_logits, gain, layer, ts):
    bsz, seq, d = xb.shape
    nh = w_heads.shape[0]
    dk = HGRN_DK
    ts = min(ts, seq)
    slots = lb_logits.shape[0]
    lbl = lb_logits.reshape(slots, nh, dk).transpose(1, 0, 2)
    buf = pltpu.VMEM((ts, dk), F32)
    return pl.pallas_call(
        functools.partial(_hgrn_kernel, layer=layer, ts=ts),
        grid=(bsz, seq // ts, nh),
        in_specs=[pl.BlockSpec((1, ts, d), lambda b, s, h: (b, s, 0)),
                  pl.BlockSpec((1, d, 4 * dk), lambda b, s, h: (h, 0, 0)),
                  pl.BlockSpec((1, slots, dk), lambda b, s, h: (h, 0, 0)),
                  pl.BlockSpec((1, dk), lambda b, s, h: (0, 0))],
        out_specs=pl.BlockSpec((1, ts, dk), lambda b, s, h: (b, s, h)),
        out_shape=jax.ShapeDtypeStruct((bsz, seq, d), MXU_DTYPE),
        scratch_shapes=[pltpu.VMEM((nh, dk, dk), F32), buf, buf, buf, buf, buf, buf],
        compiler_params=_params(("parallel", "arbitrary", "arbitrary")),
        name="hgrn2",
    )(xb, w_heads, lbl, gain.reshape(1, dk))


def _attn_kernel(q_ref, kp_ref, kc_ref, vp_ref, vc_ref, bias_ref, sink_ref, o_ref, *, group):
    kvh = pl.program_id(0)
    n = pl.program_id(2)
    blk = ATTN_BLOCK
    kk = jnp.concatenate([kp_ref[0, 0], kc_ref[0, 0]], axis=0)
    vv = jnp.concatenate([vp_ref[0, 0], vc_ref[0, 0]], axis=0)
    si = lax.broadcasted_iota(jnp.int32, (1, 2 * blk), 1)
    first = jnp.where((si >= blk) | (n > 0), 0.0, NEG_INF)
    outs = []
    for g in range(group):
        qh = q_ref[0, :, g * HEAD_DIM:(g + 1) * HEAD_DIM] * ATTN_SCALE
        s = _dot_nt(qh, kk) + bias_ref[g] + first
        sink = sink_ref[kvh * group + g]
        m = jnp.maximum(jnp.max(s, axis=-1, keepdims=True), sink)
        p = jnp.exp(s - m)
        w = p / (jnp.sum(p, axis=-1, keepdims=True) + jnp.exp(sink - m))
        outs.append(_dot(w.astype(MXU_DTYPE), vv))
    o_ref[0] = jnp.concatenate(outs, axis=-1).astype(o_ref.dtype)


def _attention(q, k_sh, v_sh, sinks):
    bsz, seq, dq = q.shape
    group = dq // (KV_HEADS * HEAD_DIM)
    blk = ATTN_BLOCK
    gw = group * HEAD_DIM
    cur = pl.BlockSpec((1, 1, blk, HEAD_DIM), lambda k, b, n: (b, k, n, 0))
    prev = pl.BlockSpec((1, 1, blk, HEAD_DIM), lambda k, b, n: (b, k, jnp.maximum(n - 1, 0), 0))
    nheads = KV_HEADS * group
    slopes = jnp.asarray([2.0 ** (-8.0 * h / nheads) for h in range(1, nheads + 1)], F32)
    dist = (jnp.arange(blk)[:, None] + blk - jnp.arange(2 * blk)[None, :])
    bias = jnp.where((dist >= 0) & (dist < blk),
                     -(slopes[:, None, None] * dist.astype(F32)[None]), NEG_INF)
    return pl.pallas_call(
        functools.partial(_attn_kernel, group=group),
        grid=(KV_HEADS, bsz, seq // blk),
        in_specs=[pl.BlockSpec((1, blk, gw), lambda k, b, n: (b, n, k)),
                  prev, cur, prev, cur,
                  pl.BlockSpec((group, blk, 2 * blk), lambda k, b, n: (k, 0, 0)),
                  pl.BlockSpec(memory_space=pltpu.SMEM)],
        out_specs=pl.BlockSpec((1, blk, gw), lambda k, b, n: (b, n, k)),
        out_shape=jax.ShapeDtypeStruct((bsz, seq, dq), MXU_DTYPE),
        compiler_params=_params(("parallel", "parallel", "arbitrary")),
        name="swa_attention",
    )(q, k_sh, k_sh, v_sh, v_sh, bias, sinks)


def _top16(s):
    rid = lax.broadcasted_iota(jnp.int32, (PEER_TOPK, s.shape[1]), 0)
    tops = jnp.zeros((PEER_TOPK, s.shape[1]), F32)
    work = s
    for kth in range(PEER_TOPK):
        m = jnp.max(work, axis=0, keepdims=True)
        tops = jnp.where(rid == kth, m, tops)
        work = jnp.where(work == m, -jnp.inf, work)
    return tops


def _peer_sel_kernel(x_ref, wq_ref, sub_ref, s1_ref, s2_ref, st_ref, *, tb):
    q = _dot(x_ref[...].astype(MXU_DTYPE), wq_ref[...]).astype(MXU_DTYPE)
    s1 = _dot_nt(sub_ref[0, 0], q[:, :PEER_HALF])
    s2 = _dot_nt(sub_ref[0, 1], q[:, PEER_HALF:])
    for lb in range(tb // LANES):
        a1 = s1[:, lb * LANES:(lb + 1) * LANES]
        a2 = s2[:, lb * LANES:(lb + 1) * LANES]
        s1_ref[lb, 0] = a1
        s2_ref[lb, 0] = a2
        t1 = _top16(a1)
        t2 = _top16(a2)
        cands = [t1[0:1] + t2]
        for a in range(1, SUBLANES):
            cands.append(t1[a:a + 1] + t2[0:SUBLANES])
        cands.append(t1[SUBLANES:] + t2[0:1])
        cand = jnp.concatenate(cands, axis=0)
        best = _top16(cand)
        top = best[0:1]
        tau = best[PEER_TOPK - 1:PEER_TOPK]
        z = jnp.sum(jnp.where(cand >= tau, jnp.exp(cand - top), 0.0), axis=0, keepdims=True)
        zero = jnp.zeros_like(z)
        st_ref[lb, 0] = jnp.concatenate(
            [tau, t1[0:1], t2[0:1], 1.0 / z, zero, zero, zero, zero], axis=0)


def _peer_select(x, w_q, subkeys, tb):
    t, d = x.shape
    tb = min(tb, t)
    nlb = tb // LANES
    qd = 2 * PEER_HALF
    blk = lambda rows: pl.BlockSpec((nlb, 1, rows, LANES), lambda i, h: (i, h, 0, 0))
    shp = lambda rows: jax.ShapeDtypeStruct((t // LANES, PEER_HEADS, rows, LANES), F32)
    return pl.pallas_call(
        functools.partial(_peer_sel_kernel, tb=tb),
        grid=(t // tb, PEER_HEADS),
        in_specs=[pl.BlockSpec((tb, d), lambda i, h: (i, 0)),
                  pl.BlockSpec((d, qd), lambda i, h: (0, h)),
                  pl.BlockSpec((1, 2, PEER_NKEYS, PEER_HALF), lambda i, h: (h, 0, 0, 0))],
        out_specs=[blk(PEER_NKEYS), blk(PEER_NKEYS), blk(SUBLANES)],
        out_shape=[shp(PEER_NKEYS), shp(PEER_NKEYS), shp(SUBLANES)],
        compiler_params=_params(("parallel", "arbitrary")),
        name="peer_select",
    )(x, w_q, subkeys)


def _gelu(x):
    return 0.5 * x * (1.0 + lax.erf(x * (1.0 / math.sqrt(2.0))))


def _quantize_rows(a):
    a = a.astype(F32)
    amax = jnp.max(jnp.abs(a), axis=1, keepdims=True)
    scale = jnp.where(amax > 0.0, FP8_ROW_TOP / amax, 1.0)
    return (a * scale).astype(PEER_HIDDEN_DTYPE), (1.0 / scale)[:, 0]


def _peer_dense_kernel(x_ref, u_ref, v_ref, sx_ref, su_ref, s1_ref, s2_ref, st_ref, o_ref,
                       e2_ref, w1_ref, g_ref, at_ref, *, tb, ec):
    e = pl.program_id(1)
    nlb = tb // LANES
    nsub = ec // PEER_NKEYS

    @pl.when(e == 0)
    def _():
        o_ref[...] = jnp.zeros(o_ref.shape, F32)
        for lb in range(nlb):
            for h in range(PEER_HEADS):
                e2_ref[lb, h] = jnp.exp(s2_ref[lb, h] - st_ref[lb, h, 2:3, :])
                w1_ref[lb, h] = jnp.exp(s1_ref[lb, h] - st_ref[lb, h, 1:2, :]) * st_ref[lb, h, 3:4, :]

    for ii in range(nsub):
        i = e * nsub + ii
        for lb in range(nlb):
            lanes = slice(lb * LANES, (lb + 1) * LANES)
            for kt in range(PEER_NKEYS // GATE_ROWS):
                keys = slice(kt * GATE_ROWS, (kt + 1) * GATE_ROWS)
                gate = jnp.zeros((GATE_ROWS, LANES), F32)
                for h in range(PEER_HEADS):
                    tau = st_ref[lb, h, 0:1, :]
                    cs = s1_ref[lb, h, pl.ds(i, 1), :] + s2_ref[lb, h, keys, :]
                    val = e2_ref[lb, h, keys, :] * w1_ref[lb, h, pl.ds(i, 1), :]
                    gate = gate + jnp.where(cs >= tau, val, 0.0)
                g_ref[ii * PEER_NKEYS + kt * GATE_ROWS:ii * PEER_NKEYS + (kt + 1) * GATE_ROWS,
                      lanes] = gate

    ht = _dot_nt(u_ref[...], x_ref[...])
    su = su_ref[...]
    for lb in range(nlb):
        lanes = slice(lb * LANES, (lb + 1) * LANES)
        hid = ht[:, lanes] * su * sx_ref[:, lanes]
        at_ref[:, lanes] = (_gelu(hid) * g_ref[:, lanes]).astype(at_ref.dtype)
    o_ref[...] += _dot_tn(at_ref[...], v_ref[...])


def _peer_dense(x, sx, u, su, v, s1, s2, st, tb, ec):
    t, d = x.shape
    tb = min(tb, t)
    nlb = tb // LANES
    ne = u.shape[0] // ec
    once = pl.Buffered(1)
    sblk = lambda rows: pl.BlockSpec((nlb, PEER_HEADS, rows, LANES), lambda i, e: (i, 0, 0, 0),
                                     pipeline_mode=once)
    return pl.pallas_call(
        functools.partial(_peer_dense_kernel, tb=tb, ec=ec),
        grid=(t // tb, ne),
        in_specs=[pl.BlockSpec((tb, d), lambda i, e: (i, 0), pipeline_mode=once),
                  pl.BlockSpec((ec, d), lambda i, e: (e, 0)),
                  pl.BlockSpec((ec, d), lambda i, e: (e, 0)),
                  pl.BlockSpec((1, tb), lambda i, e: (0, i), pipeline_mode=once),
                  pl.BlockSpec((ec, LANES), lambda i, e: (e, 0)),
                  sblk(PEER_NKEYS), sblk(PEER_NKEYS), sblk(SUBLANES)],
        out_specs=pl.BlockSpec((tb, d), lambda i, e: (i, 0)),
        out_shape=jax.ShapeDtypeStruct((t, d), F32),
        scratch_shapes=[pltpu.VMEM((nlb, PEER_HEADS, PEER_NKEYS, LANES), F32),
                        pltpu.VMEM((nlb, PEER_HEADS, PEER_NKEYS, LANES), F32),
                        pltpu.VMEM((ec, tb), F32),
                        pltpu.VMEM((ec, tb), MXU_DTYPE)],
        compiler_params=_params(("parallel", "arbitrary")),
        name="peer_dense",
    )(x, u, v, sx.reshape(1, t), jnp.broadcast_to(su[:, None], (u.shape[0], LANES)), s1, s2, st)


def _peer_layer(x, xb, w_q, subkeys, u, v, gain, bias, with_copy):
    s1, s2, st = _peer_select(xb, w_q, subkeys, tb=512)
    xq, sx = _quantize_rows(x)
    uq, su = _quantize_rows(u)
    y = _peer_dense(xq, sx, uq, su, v, s1, s2, st, tb=512, ec=512)
    return _res_ln(x, y, gain, bias, tm=256, with_copy=with_copy)


def kernel(x, hgrn_w_in, hgrn_lb_logits, hgrn_norm_gain, hgrn_w_out, kv_w, attn_w_q, attn_sinks,
           attn_w_out, peer_w_q, peer_subkeys, peer_u, peer_v, ln_gain, ln_bias):
    bsz, seq, d = x.shape
    t = bsz * seq
    cdt = MXU_DTYPE
    dk = HGRN_DK
    nh = d // dk

    w_in = hgrn_w_in[0].astype(cdt).reshape(d, 4, nh, dk).transpose(2, 0, 1, 3).reshape(nh, d, 4 * dk)
    o = _hgrn(x.astype(cdt), w_in, hgrn_lb_logits, hgrn_norm_gain[0], layer=0, ts=512)
    xt = x.reshape(t, d)
    x1, x1b = _matmul_ln(o.reshape(t, d), hgrn_w_out[0].astype(cdt), xt, ln_gain[0, 0], ln_bias[0, 0],
                         tm=512, tn=512)
    x2, x2b = _peer_layer(x1, x1b, peer_w_q[0].astype(cdt), peer_subkeys[0].astype(cdt),
                          peer_u[0], peer_v[0].astype(cdt), ln_gain[0, 1], ln_bias[0, 1],
                          with_copy=True)

    nkv = KV_HEADS * HEAD_DIM
    kv = _matmul(x2b, kv_w.astype(cdt), cdt, tm=512, tn=1024)
    to_heads = lambda a: a.reshape(bsz, seq, KV_HEADS, HEAD_DIM).transpose(0, 2, 1, 3)
    k_sh = to_heads(kv[:, :nkv])
    v_sh = to_heads(kv[:, nkv:])

    q = _matmul(x2b, attn_w_q[0].astype(cdt), cdt, tm=512, tn=1024)
    att = _attention(q.reshape(bsz, seq, -1), k_sh, v_sh, attn_sinks[0].astype(F32))
    x3, x3b = _matmul_ln(att.reshape(t, -1), attn_w_out[0].astype(cdt), x2, ln_gain[1, 0], ln_bias[1, 0],
                         tm=512, tn=512)
    (x4,) = _peer_layer(x3, x3b, peer_w_q[1].astype(cdt), peer_subkeys[1].astype(cdt),
                        peer_u[1], peer_v[1].astype(cdt), ln_gain[1, 1], ln_bias[1, 1],
                        with_copy=False)
    return x4.reshape(bsz, seq, d)
```

```python
import functools
import math

import jax
import jax.numpy as jnp
from jax import lax
from jax.experimental import pallas as pl
from jax.experimental.pallas import tpu as pltpu

F32 = jnp.float32
MXU_DTYPE = jnp.bfloat16

DEPTH = 2
ALPHA = (2.0 * DEPTH) ** 0.25
LN_EPS = 1e-5
RMS_EPS = 1e-6

HGRN_DK = 128
HGRN_CHUNK = 32

HEAD_DIM = 64
KV_HEADS = 8
ATTN_BLOCK = 128
ATTN_SCALE = HEAD_DIM ** -0.5
NEG_INF = -1e30

PEER_HEADS = 8
PEER_NKEYS = 128
PEER_TOPK = 16
PEER_HALF = 128
GATE_ROWS = 32

LANES = 128
SUBLANES = 8
VMEM_LIMIT = 60 * 1024 * 1024


def _params(sem, flags=None):
    return pltpu.CompilerParams(dimension_semantics=sem, vmem_limit_bytes=VMEM_LIMIT, flags=flags)


def _sigmoid(x):
    return 1.0 / (1.0 + jnp.exp(-x))


def _dot(a, b):
    return jnp.dot(a, b, preferred_element_type=F32)


def _dot_nt(a, b):
    return lax.dot_general(a, b, (((1,), (1,)), ((), ())), preferred_element_type=F32)


def _dot_tn(a, b):
    return lax.dot_general(a, b, (((0,), (0,)), ((), ())), preferred_element_type=F32)


def _mm_kernel(a_ref, w_ref, o_ref):
    a = a_ref[...].astype(MXU_DTYPE)
    o_ref[...] = _dot(a, w_ref[...]).astype(o_ref.dtype)


def _matmul(a, w, out_dtype, tm, tn):
    m, k = a.shape
    n = w.shape[1]
    tm, tn = min(tm, m), min(tn, n)
    return pl.pallas_call(
        _mm_kernel,
        grid=(m // tm, n // tn),
        in_specs=[pl.BlockSpec((tm, k), lambda i, j: (i, 0)),
                  pl.BlockSpec((k, tn), lambda i, j: (0, j))],
        out_specs=pl.BlockSpec((tm, tn), lambda i, j: (i, j)),
        out_shape=jax.ShapeDtypeStruct((m, n), out_dtype),
        compiler_params=_params(("parallel", "arbitrary")),
        name="matmul",
    )(a, w)


def _ln_rows(z, g, b):
    mu = jnp.mean(z, axis=-1, keepdims=True)
    zc = z - mu
    var = jnp.mean(zc * zc, axis=-1, keepdims=True)
    return zc * lax.rsqrt(var + LN_EPS) * g + b


def _mm_ln_kernel(a_ref, w_ref, res_ref, g_ref, b_ref, o_ref, ob_ref, *, tn, nj, rows):
    j = pl.program_id(1)
    y = _dot(a_ref[...].astype(MXU_DTYPE), w_ref[...]) + ALPHA * res_ref[...]
    for jj in range(nj):
        @pl.when(j == jj)
        def _(jj=jj):
            o_ref[:, jj * tn:(jj + 1) * tn] = y

    @pl.when(j == nj - 1)
    def _():
        g = g_ref[...]
        b = b_ref[...]

        def body(r, carry):
            sl = pl.ds(pl.multiple_of(r * rows, rows), rows)
            z = _ln_rows(o_ref[sl, :], g, b)
            o_ref[sl, :] = z
            ob_ref[sl, :] = z.astype(ob_ref.dtype)
            return carry

        lax.fori_loop(0, o_ref.shape[0] // rows, body, 0)


def _matmul_ln(a, w, res, gain, bias, tm, tn):
    m, k = a.shape
    n = w.shape[1]
    tm, tn = min(tm, m), min(tn, n)
    nj = n // tn
    full = pl.BlockSpec((tm, n), lambda i, j: (i, 0))
    return pl.pallas_call(
        functools.partial(_mm_ln_kernel, tn=tn, nj=nj, rows=2 * SUBLANES),
        grid=(m // tm, nj),
        in_specs=[pl.BlockSpec((tm, k), lambda i, j: (i, 0)),
                  pl.BlockSpec((k, tn), lambda i, j: (0, j)),
                  pl.BlockSpec((tm, tn), lambda i, j: (i, j)),
                  pl.BlockSpec((1, n), lambda i, j: (0, 0)),
                  pl.BlockSpec((1, n), lambda i, j: (0, 0))],
        out_specs=[full, full],
        out_shape=[jax.ShapeDtypeStruct((m, n), F32), jax.ShapeDtypeStruct((m, n), MXU_DTYPE)],
        compiler_params=_params(("parallel", "arbitrary")),
        name="matmul_ln",
    )(a, w, res, gain.reshape(1, n), bias.reshape(1, n))


def _res_ln_kernel(x_ref, y_ref, g_ref, b_ref, *o_refs):
    z = _ln_rows(ALPHA * x_ref[...] + y_ref[...], g_ref[...], b_ref[...])
    for o_ref in o_refs:
        o_ref[...] = z.astype(o_ref.dtype)


def _res_ln(x, y, gain, bias, tm, with_copy):
    m, n = x.shape
    tm = min(tm, m)
    row = pl.BlockSpec((tm, n), lambda i: (i, 0))
    vec = pl.BlockSpec((1, n), lambda i: (0, 0))
    dtypes = (F32, MXU_DTYPE) if with_copy else (F32,)
    return pl.pallas_call(
        _res_ln_kernel,
        grid=(m // tm,),
        in_specs=[row, row, vec, vec],
        out_specs=[row] * len(dtypes),
        out_shape=[jax.ShapeDtypeStruct((m, n), dt) for dt in dtypes],
        compiler_params=_params(("parallel",)),
        name="res_ln",
    )(x, y, gain.reshape(1, n), bias.reshape(1, n))


def _hgrn_kernel(x_ref, w_ref, lbl_ref, gain_ref, o_ref,
                 state_ref, qg_ref, kg_ref, k_ref, b_ref, v_ref, g_ref, *, layer, ts, hp):
    sblk = pl.program_id(1)
    hb = pl.program_id(2)
    dk = HGRN_DK
    c = HGRN_CHUNK

    @pl.when(sblk == 0)
    def _():
        for j in range(hp):
            state_ref[hb * hp + j] = jnp.zeros((dk, dk), F32)

    x = x_ref[0]
    row = lax.broadcasted_iota(jnp.int32, (ts, dk), 0) % c
    for j in range(hp):
        lg = lbl_ref[j]
        ex = jnp.exp(lg - jnp.max(lg, axis=0, keepdims=True))
        lb = jnp.sum(ex[:layer + 1], axis=0, keepdims=True) / jnp.sum(ex, axis=0, keepdims=True)

        proj = _dot(x, w_ref[j])
        q_raw = proj[:, 0 * dk:1 * dk]
        f_raw = proj[:, 1 * dk:2 * dk]
        f = lb + (1.0 - lb) * _sigmoid(f_raw)
        k = 1.0 - f
        b = jnp.log(f)
        sh = 1
        while sh < c:
            b = b + jnp.where(row >= sh, pltpu.roll(b, sh, axis=0), 0.0)
            sh *= 2
        qg_ref[j] = q_raw * _sigmoid(q_raw) * jnp.exp(b)
        kg_ref[j] = k * jnp.exp(-b)
        k_ref[j] = k
        b_ref[j] = b
        v_ref[j] = proj[:, 2 * dk:3 * dk]
        g_ref[j] = proj[:, 3 * dk:4 * dk]

    causal = (lax.broadcasted_iota(jnp.int32, (c, c), 0)
              >= lax.broadcasted_iota(jnp.int32, (c, c), 1))
    gain = gain_ref[...]

    def chunk(j, ci, st):
        sl = slice(ci * c, (ci + 1) * c)
        qg = qg_ref[j, sl, :].astype(MXU_DTYPE)
        vv = v_ref[j, sl, :].astype(MXU_DTYPE)
        bc = b_ref[j, sl, :]
        bl = bc[c - 1:c, :]
        a = jnp.where(causal, _dot_nt(qg, kg_ref[j, sl, :].astype(MXU_DTYPE)), 0.0)
        o = _dot(a.astype(MXU_DTYPE), vv) + _dot_nt(qg, st.astype(MXU_DTYPE))
        kd = (k_ref[j, sl, :] * jnp.exp(bl - bc)).astype(MXU_DTYPE)
        st = st * jnp.exp(bl) + _dot_tn(vv, kd)
        o = o * lax.rsqrt(jnp.mean(o * o, axis=-1, keepdims=True) + RMS_EPS) * gain
        gr = g_ref[j, sl, :]
        o_ref[0, sl, j * dk:(j + 1) * dk] = (o * (gr * _sigmoid(gr))).astype(o_ref.dtype)
        return st

    sts = [state_ref[hb * hp + j] for j in range(hp)]
    for ci in range(ts // c):
        sts = [chunk(j, ci, sts[j]) for j in range(hp)]
    for j in range(hp):
        state_ref[hb * hp + j] = sts[j]


def _hgrn(xb, w_heads, lb_logits, gain, layer, ts, hp):
    bsz, seq, d = xb.shape
    nh = w_heads.shape[0]
    dk = HGRN_DK
    ts = min(ts, seq)
    slots = lb_logits.shape[0]
    lbl = lb_logits.reshape(slots, nh, dk).transpose(1, 0, 2)
    buf = pltpu.VMEM((hp, ts, dk), F32)
    return pl.pallas_call(
        functools.partial(_hgrn_kernel, layer=layer, ts=ts, hp=hp),
        grid=(bsz, seq // ts, nh // hp),
        in_specs=[pl.BlockSpec((1, ts, d), lambda b, s, h: (b, s, 0)),
                  pl.BlockSpec((hp, d, 4 * dk), lambda b, s, h: (h, 0, 0)),
                  pl.BlockSpec((hp, slots, dk), lambda b, s, h: (h, 0, 0)),
                  pl.BlockSpec((1, dk), lambda b, s, h: (0, 0))],
        out_specs=pl.BlockSpec((1, ts, hp * dk), lambda b, s, h: (b, s, h)),
        out_shape=jax.ShapeDtypeStruct((bsz, seq, d), MXU_DTYPE),
        scratch_shapes=[pltpu.VMEM((nh, dk, dk), F32), buf, buf, buf, buf, buf, buf],
        compiler_params=_params(("parallel", "arbitrary", "arbitrary")),
        name="hgrn2",
    )(xb, w_heads, lbl, gain.reshape(1, dk))


def _attn_kernel(q_ref, kp_ref, kc_ref, vp_ref, vc_ref, bias_ref, sink_ref, o_ref, *, group, kvp):
    kvh = pl.program_id(0)
    n = pl.program_id(2)
    blk = ATTN_BLOCK
    si = lax.broadcasted_iota(jnp.int32, (1, 2 * blk), 1)
    first = jnp.where((si >= blk) | (n > 0), 0.0, NEG_INF)
    outs = []
    for j in range(kvp):
        kk = jnp.concatenate([kp_ref[0, j], kc_ref[0, j]], axis=0)
        vv = jnp.concatenate([vp_ref[0, j], vc_ref[0, j]], axis=0)
        for g in range(group):
            hq = j * group + g
            qh = q_ref[0, :, hq * HEAD_DIM:(hq + 1) * HEAD_DIM] * ATTN_SCALE
            s = _dot_nt(qh, kk) + bias_ref[hq] + first
            sink = sink_ref[kvh * kvp * group + hq]
            m = jnp.maximum(jnp.max(s, axis=-1, keepdims=True), sink)
            p = jnp.exp(s - m)
            w = p / (jnp.sum(p, axis=-1, keepdims=True) + jnp.exp(sink - m))
            outs.append(_dot(w.astype(MXU_DTYPE), vv))
    o_ref[0] = jnp.concatenate(outs, axis=-1).astype(o_ref.dtype)


def _attention(q, k_sh, v_sh, sinks, kvp):
    bsz, seq, dq = q.shape
    group = dq // (KV_HEADS * HEAD_DIM)
    blk = ATTN_BLOCK
    gw = kvp * group * HEAD_DIM
    cur = pl.BlockSpec((1, kvp, blk, HEAD_DIM), lambda k, b, n: (b, k, n, 0))
    prev = pl.BlockSpec((1, kvp, blk, HEAD_DIM), lambda k, b, n: (b, k, jnp.maximum(n - 1, 0), 0))
    nheads = KV_HEADS * group
    slopes = jnp.asarray([2.0 ** (-8.0 * h / nheads) for h in range(1, nheads + 1)], F32)
    dist = (jnp.arange(blk)[:, None] + blk - jnp.arange(2 * blk)[None, :])
    bias = jnp.where((dist >= 0) & (dist < blk),
                     -(slopes[:, None, None] * dist.astype(F32)[None]), NEG_INF)
    return pl.pallas_call(
        functools.partial(_attn_kernel, group=group, kvp=kvp),
        grid=(KV_HEADS // kvp, bsz, seq // blk),
        in_specs=[pl.BlockSpec((1, blk, gw), lambda k, b, n: (b, n, k)),
                  prev, cur, prev, cur,
                  pl.BlockSpec((kvp * group, blk, 2 * blk), lambda k, b, n: (k, 0, 0)),
                  pl.BlockSpec(memory_space=pltpu.SMEM)],
        out_specs=pl.BlockSpec((1, blk, gw), lambda k, b, n: (b, n, k)),
        out_shape=jax.ShapeDtypeStruct((bsz, seq, dq), MXU_DTYPE),
        compiler_params=_params(("parallel", "parallel", "arbitrary")),
        name="swa_attention",
    )(q, k_sh, k_sh, v_sh, v_sh, bias, sinks)


def _top16(s):
    rid = lax.broadcasted_iota(jnp.int32, (PEER_TOPK, s.shape[1]), 0)
    tops = jnp.zeros((PEER_TOPK, s.shape[1]), F32)
    work = s
    for kth in range(PEER_TOPK):
        m = jnp.max(work, axis=0, keepdims=True)
        tops = jnp.where(rid == kth, m, tops)
        work = jnp.where(work == m, -jnp.inf, work)
    return tops


def _peer_sel_kernel(x_ref, wq_ref, sub_ref, s1_ref, s2_ref, st_ref, *, tb):
    q = _dot(x_ref[...].astype(MXU_DTYPE), wq_ref[...]).astype(MXU_DTYPE)
    s1 = _dot_nt(sub_ref[0, 0], q[:, :PEER_HALF])
    s2 = _dot_nt(sub_ref[0, 1], q[:, PEER_HALF:])
    for lb in range(tb // LANES):
        a1 = s1[:, lb * LANES:(lb + 1) * LANES]
        a2 = s2[:, lb * LANES:(lb + 1) * LANES]
        s1_ref[lb, 0] = a1
        s2_ref[lb, 0] = a2
        t1 = _top16(a1)
        t2 = _top16(a2)
        cands = [t1[0:1] + t2]
        for a in range(1, SUBLANES):
            cands.append(t1[a:a + 1] + t2[0:SUBLANES])
        cands.append(t1[SUBLANES:] + t2[0:1])
        cand = jnp.concatenate(cands, axis=0)
        best = _top16(cand)
        top = best[0:1]
        tau = best[PEER_TOPK - 1:PEER_TOPK]
        z = jnp.sum(jnp.where(cand >= tau, jnp.exp(cand - top), 0.0), axis=0, keepdims=True)
        zero = jnp.zeros_like(z)
        st_ref[lb, 0] = jnp.concatenate(
            [tau, t1[0:1], t2[0:1], 1.0 / z, zero, zero, zero, zero], axis=0)


def _peer_select(x, w_q, subkeys, tb):
    t, d = x.shape
    tb = min(tb, t)
    nlb = tb // LANES
    qd = 2 * PEER_HALF
    blk = lambda rows: pl.BlockSpec((nlb, 1, rows, LANES), lambda i, h: (i, h, 0, 0))
    shp = lambda rows: jax.ShapeDtypeStruct((t // LANES, PEER_HEADS, rows, LANES), F32)
    return pl.pallas_call(
        functools.partial(_peer_sel_kernel, tb=tb),
        grid=(t // tb, PEER_HEADS),
        in_specs=[pl.BlockSpec((tb, d), lambda i, h: (i, 0)),
                  pl.BlockSpec((d, qd), lambda i, h: (0, h)),
                  pl.BlockSpec((1, 2, PEER_NKEYS, PEER_HALF), lambda i, h: (h, 0, 0, 0))],
        out_specs=[blk(PEER_NKEYS), blk(PEER_NKEYS), blk(SUBLANES)],
        out_shape=[shp(PEER_NKEYS), shp(PEER_NKEYS), shp(SUBLANES)],
        compiler_params=_params(("parallel", "arbitrary")),
        name="peer_select",
    )(x, w_q, subkeys)


def _gelu(x):
    return 0.5 * x * (1.0 + lax.erf(x * (1.0 / math.sqrt(2.0))))


def _peer_dense_kernel(x_ref, u_ref, v_ref, s1_ref, s2_ref, st_ref, o_ref,
                       e2_ref, w1_ref, g_ref, at_ref, *, tb, ec):
    e = pl.program_id(1)
    nlb = tb // LANES
    nsub = ec // PEER_NKEYS

    @pl.when(e == 0)
    def _():
        o_ref[...] = jnp.zeros(o_ref.shape, F32)
        for lb in range(nlb):
            for h in range(PEER_HEADS):
                e2_ref[lb, h] = jnp.exp(s2_ref[lb, h] - st_ref[lb, h, 2:3, :])
                w1_ref[lb, h] = jnp.exp(s1_ref[lb, h] - st_ref[lb, h, 1:2, :]) * st_ref[lb, h, 3:4, :]

    for ii in range(nsub):
        i = e * nsub + ii
        for lb in range(nlb):
            lanes = slice(lb * LANES, (lb + 1) * LANES)
            for kt in range(PEER_NKEYS // GATE_ROWS):
                keys = slice(kt * GATE_ROWS, (kt + 1) * GATE_ROWS)
                gate = jnp.zeros((GATE_ROWS, LANES), F32)
                for h in range(PEER_HEADS):
                    tau = st_ref[lb, h, 0:1, :]
                    cs = s1_ref[lb, h, pl.ds(i, 1), :] + s2_ref[lb, h, keys, :]
                    val = e2_ref[lb, h, keys, :] * w1_ref[lb, h, pl.ds(i, 1), :]
                    gate = gate + jnp.where(cs >= tau, val, 0.0)
                g_ref[ii * PEER_NKEYS + kt * GATE_ROWS:ii * PEER_NKEYS + (kt + 1) * GATE_ROWS,
                      lanes] = gate

    ht = _dot_nt(u_ref[...], x_ref[...])
    at_ref[...] = (_gelu(ht) * g_ref[...]).astype(at_ref.dtype)
    o_ref[...] += _dot_tn(at_ref[...], v_ref[...])


def _peer_dense(x, u, v, s1, s2, st, tb, ec):
    t, d = x.shape
    tb = min(tb, t)
    nlb = tb // LANES
    ne = u.shape[0] // ec
    once = pl.Buffered(1)
    sblk = lambda rows: pl.BlockSpec((nlb, PEER_HEADS, rows, LANES), lambda i, e: (i, 0, 0, 0),
                                     pipeline_mode=once)
    return pl.pallas_call(
        functools.partial(_peer_dense_kernel, tb=tb, ec=ec),
        grid=(t // tb, ne),
        in_specs=[pl.BlockSpec((tb, d), lambda i, e: (i, 0), pipeline_mode=once),
                  pl.BlockSpec((ec, d), lambda i, e: (e, 0)),
                  pl.BlockSpec((ec, d), lambda i, e: (e, 0)),
                  sblk(PEER_NKEYS), sblk(PEER_NKEYS), sblk(SUBLANES)],
        out_specs=pl.BlockSpec((tb, d), lambda i, e: (i, 0)),
        out_shape=jax.ShapeDtypeStruct((t, d), F32),
        scratch_shapes=[pltpu.VMEM((nlb, PEER_HEADS, PEER_NKEYS, LANES), F32),
                        pltpu.VMEM((nlb, PEER_HEADS, PEER_NKEYS, LANES), F32),
                        pltpu.VMEM((ec, tb), F32),
                        pltpu.VMEM((ec, tb), MXU_DTYPE)],
        compiler_params=_params(("parallel", "arbitrary")),
        name="peer_dense",
    )(x, u, v, s1, s2, st)


def _peer_layer(x, xb, w_q, subkeys, u, v, gain, bias, with_copy):
    s1, s2, st = _peer_select(xb, w_q, subkeys, tb=512)
    y = _peer_dense(xb, u, v, s1, s2, st, tb=512, ec=512)
    return _res_ln(x, y, gain, bias, tm=256, with_copy=with_copy)


def kernel(x, hgrn_w_in, hgrn_lb_logits, hgrn_norm_gain, hgrn_w_out, kv_w, attn_w_q, attn_sinks,
           attn_w_out, peer_w_q, peer_subkeys, peer_u, peer_v, ln_gain, ln_bias):
    bsz, seq, d = x.shape
    t = bsz * seq
    cdt = MXU_DTYPE
    dk = HGRN_DK
    nh = d // dk

    w_in = hgrn_w_in[0].astype(cdt).reshape(d, 4, nh, dk).transpose(2, 0, 1, 3).reshape(nh, d, 4 * dk)
    o = _hgrn(x.astype(cdt), w_in, hgrn_lb_logits, hgrn_norm_gain[0], layer=0, ts=512, hp=4)
    xt = x.reshape(t, d)
    x1, x1b = _matmul_ln(o.reshape(t, d), hgrn_w_out[0].astype(cdt), xt, ln_gain[0, 0], ln_bias[0, 0],
                         tm=512, tn=1024)
    x2, x2b = _peer_layer(x1, x1b, peer_w_q[0].astype(cdt), peer_subkeys[0].astype(cdt),
                          peer_u[0].astype(cdt), peer_v[0].astype(cdt), ln_gain[0, 1], ln_bias[0, 1],
                          with_copy=True)

    nkv = KV_HEADS * HEAD_DIM
    kv = _matmul(x2b, kv_w.astype(cdt), cdt, tm=512, tn=1024)
    to_heads = lambda a: a.reshape(bsz, seq, KV_HEADS, HEAD_DIM).transpose(0, 2, 1, 3)
    k_sh = to_heads(kv[:, :nkv])
    v_sh = to_heads(kv[:, nkv:])

    q = _matmul(x2b, attn_w_q[0].astype(cdt), cdt, tm=512, tn=1024)
    att = _attention(q.reshape(bsz, seq, -1), k_sh, v_sh, attn_sinks[0].astype(F32), kvp=4)
    x3, x3b = _matmul_ln(att.reshape(t, -1), attn_w_out[0].astype(cdt), x2, ln_gain[1, 0], ln_bias[1, 0],
                         tm=512, tn=1024)
    (x4,) = _peer_layer(x3, x3b, peer_w_q[1].astype(cdt), peer_subkeys[1].astype(cdt),
                        peer_u[1].astype(cdt), peer_v[1].astype(cdt), ln_gain[1, 1], ln_bias[1, 1],
                        with_copy=False)
    return x4.reshape(bsz, seq, d)
```

```python
import functools
import math

import jax
import jax.numpy as jnp
from jax import lax
from jax.experimental import pallas as pl
from jax.experimental.pallas import tpu as pltpu

F32 = jnp.float32
MXU_DTYPE = jnp.bfloat16

DEPTH = 2
ALPHA = (2.0 * DEPTH) ** 0.25
LN_EPS = 1e-5
RMS_EPS = 1e-6

HGRN_DK = 128
HGRN_CHUNK = 32

HEAD_DIM = 64
KV_HEADS = 8
ATTN_BLOCK = 128
ATTN_SCALE = HEAD_DIM ** -0.5
NEG_INF = -1e30

PEER_HEADS = 8
PEER_NKEYS = 128
PEER_TOPK = 16
PEER_HALF = 128
GATE_ROWS = 32

LANES = 128
SUBLANES = 8
VMEM_LIMIT = 60 * 1024 * 1024


def _params(sem, flags=None):
    return pltpu.CompilerParams(dimension_semantics=sem, vmem_limit_bytes=VMEM_LIMIT, flags=flags)


def _sigmoid(x):
    return 1.0 / (1.0 + jnp.exp(-x))


def _dot(a, b):
    return jnp.dot(a, b, preferred_element_type=F32)


def _dot_nt(a, b):
    return lax.dot_general(a, b, (((1,), (1,)), ((), ())), preferred_element_type=F32)


def _dot_tn(a, b):
    return lax.dot_general(a, b, (((0,), (0,)), ((), ())), preferred_element_type=F32)


def _mm_kernel(a_ref, w_ref, o_ref):
    a = a_ref[...].astype(MXU_DTYPE)
    o_ref[...] = _dot(a, w_ref[...]).astype(o_ref.dtype)


def _matmul(a, w, out_dtype, tm, tn):
    m, k = a.shape
    n = w.shape[1]
    tm, tn = min(tm, m), min(tn, n)
    return pl.pallas_call(
        _mm_kernel,
        grid=(m // tm, n // tn),
        in_specs=[pl.BlockSpec((tm, k), lambda i, j: (i, 0)),
                  pl.BlockSpec((k, tn), lambda i, j: (0, j))],
        out_specs=pl.BlockSpec((tm, tn), lambda i, j: (i, j)),
        out_shape=jax.ShapeDtypeStruct((m, n), out_dtype),
        compiler_params=_params(("parallel", "arbitrary")),
        name="matmul",
    )(a, w)


def _kv_kernel(a_ref, w_ref, k_ref, v_ref):
    y = _dot(a_ref[0].astype(MXU_DTYPE), w_ref[...])
    nkv = KV_HEADS * HEAD_DIM
    for h in range(KV_HEADS):
        cols = slice(h * HEAD_DIM, (h + 1) * HEAD_DIM)
        k_ref[0, h] = y[:, cols].astype(k_ref.dtype)
        v_ref[0, h] = y[:, nkv:][:, cols].astype(v_ref.dtype)


def _kv_proj(x, w, tm):
    bsz, seq, d = x.shape
    tm = min(tm, seq)
    out = pl.BlockSpec((1, KV_HEADS, tm, HEAD_DIM), lambda b, i: (b, 0, i, 0))
    shape = jax.ShapeDtypeStruct((bsz, KV_HEADS, seq, HEAD_DIM), MXU_DTYPE)
    return pl.pallas_call(
        _kv_kernel,
        grid=(bsz, seq // tm),
        in_specs=[pl.BlockSpec((1, tm, d), lambda b, i: (b, i, 0)),
                  pl.BlockSpec(w.shape, lambda b, i: (0, 0))],
        out_specs=[out, out],
        out_shape=[shape, shape],
        compiler_params=_params(("parallel", "arbitrary")),
        name="kv_proj",
    )(x, w)


def _ln_rows(z, g, b):
    mu = jnp.mean(z, axis=-1, keepdims=True)
    zc = z - mu
    var = jnp.mean(zc * zc, axis=-1, keepdims=True)
    return zc * lax.rsqrt(var + LN_EPS) * g + b


def _mm_ln_kernel(a_ref, w_ref, res_ref, g_ref, b_ref, o_ref, ob_ref, *, tn, nj, rows):
    j = pl.program_id(1)
    y = _dot(a_ref[...].astype(MXU_DTYPE), w_ref[...]) + ALPHA * res_ref[...]
    for jj in range(nj):
        @pl.when(j == jj)
        def _(jj=jj):
            o_ref[:, jj * tn:(jj + 1) * tn] = y

    @pl.when(j == nj - 1)
    def _():
        g = g_ref[...]
        b = b_ref[...]

        def body(r, carry):
            sl = pl.ds(pl.multiple_of(r * rows, rows), rows)
            z = _ln_rows(o_ref[sl, :], g, b)
            o_ref[sl, :] = z
            ob_ref[sl, :] = z.astype(ob_ref.dtype)
            return carry

        lax.fori_loop(0, o_ref.shape[0] // rows, body, 0)


def _matmul_ln(a, w, res, gain, bias, tm, tn):
    m, k = a.shape
    n = w.shape[1]
    tm, tn = min(tm, m), min(tn, n)
    nj = n // tn
    full = pl.BlockSpec((tm, n), lambda i, j: (i, 0))
    return pl.pallas_call(
        functools.partial(_mm_ln_kernel, tn=tn, nj=nj, rows=2 * SUBLANES),
        grid=(m // tm, nj),
        in_specs=[pl.BlockSpec((tm, k), lambda i, j: (i, 0)),
                  pl.BlockSpec((k, tn), lambda i, j: (0, j)),
                  pl.BlockSpec((tm, tn), lambda i, j: (i, j)),
                  pl.BlockSpec((1, n), lambda i, j: (0, 0)),
                  pl.BlockSpec((1, n), lambda i, j: (0, 0))],
        out_specs=[full, full],
        out_shape=[jax.ShapeDtypeStruct((m, n), F32), jax.ShapeDtypeStruct((m, n), MXU_DTYPE)],
        compiler_params=_params(("parallel", "arbitrary")),
        name="matmul_ln",
    )(a, w, res, gain.reshape(1, n), bias.reshape(1, n))


def _res_ln_kernel(x_ref, y_ref, g_ref, b_ref, *o_refs):
    z = _ln_rows(ALPHA * x_ref[...] + y_ref[...], g_ref[...], b_ref[...])
    for o_ref in o_refs:
        o_ref[...] = z.astype(o_ref.dtype)


def _res_ln(x, y, gain, bias, tm, with_copy):
    m, n = x.shape
    tm = min(tm, m)
    row = pl.BlockSpec((tm, n), lambda i: (i, 0))
    vec = pl.BlockSpec((1, n), lambda i: (0, 0))
    dtypes = (F32, MXU_DTYPE) if with_copy else (F32,)
    return pl.pallas_call(
        _res_ln_kernel,
        grid=(m // tm,),
        in_specs=[row, row, vec, vec],
        out_specs=[row] * len(dtypes),
        out_shape=[jax.ShapeDtypeStruct((m, n), dt) for dt in dtypes],
        compiler_params=_params(("parallel",)),
        name="res_ln",
    )(x, y, gain.reshape(1, n), bias.reshape(1, n))


def _hgrn_kernel(x_ref, wq_ref, wf_ref, wi_ref, wg_ref, lbl_ref, gain_ref, o_ref,
                 state_ref, qg_ref, kg_ref, k_ref, b_ref, v_ref, g_ref, *, layer, ts, hp):
    sblk = pl.program_id(1)
    hb = pl.program_id(2)
    dk = HGRN_DK
    c = HGRN_CHUNK

    @pl.when(sblk == 0)
    def _():
        for j in range(hp):
            state_ref[hb * hp + j] = jnp.zeros((dk, dk), F32)

    x = x_ref[0]
    pq, pf, pi, pg = (_dot(x, w[...]) for w in (wq_ref, wf_ref, wi_ref, wg_ref))
    row = lax.broadcasted_iota(jnp.int32, (ts, dk), 0) % c
    for j in range(hp):
        cols = slice(j * dk, (j + 1) * dk)
        lg = lbl_ref[j]
        ex = jnp.exp(lg - jnp.max(lg, axis=0, keepdims=True))
        lb = jnp.sum(ex[:layer + 1], axis=0, keepdims=True) / jnp.sum(ex, axis=0, keepdims=True)

        q_raw = pq[:, cols]
        f_raw = pf[:, cols]
        f = lb + (1.0 - lb) * _sigmoid(f_raw)
        k = 1.0 - f
        b = jnp.log(f)
        sh = 1
        while sh < c:
            b = b + jnp.where(row >= sh, pltpu.roll(b, sh, axis=0), 0.0)
            sh *= 2
        qg_ref[j] = q_raw * _sigmoid(q_raw) * jnp.exp(b)
        kg_ref[j] = k * jnp.exp(-b)
        k_ref[j] = k
        b_ref[j] = b
        v_ref[j] = pi[:, cols]
        g_ref[j] = pg[:, cols]

    causal = (lax.broadcasted_iota(jnp.int32, (c, c), 0)
              >= lax.broadcasted_iota(jnp.int32, (c, c), 1))
    gain = gain_ref[...]

    def chunk(j, ci, st):
        sl = slice(ci * c, (ci + 1) * c)
        qg = qg_ref[j, sl, :].astype(MXU_DTYPE)
        vv = v_ref[j, sl, :].astype(MXU_DTYPE)
        bc = b_ref[j, sl, :]
        bl = bc[c - 1:c, :]
        a = jnp.where(causal, _dot_nt(qg, kg_ref[j, sl, :].astype(MXU_DTYPE)), 0.0)
        o = _dot(a.astype(MXU_DTYPE), vv) + _dot_nt(qg, st.astype(MXU_DTYPE))
        kd = (k_ref[j, sl, :] * jnp.exp(bl - bc)).astype(MXU_DTYPE)
        st = st * jnp.exp(bl) + _dot_tn(vv, kd)
        o = o * lax.rsqrt(jnp.mean(o * o, axis=-1, keepdims=True) + RMS_EPS) * gain
        gr = g_ref[j, sl, :]
        o_ref[0, sl, j * dk:(j + 1) * dk] = (o * (gr * _sigmoid(gr))).astype(o_ref.dtype)
        return st

    sts = [state_ref[hb * hp + j] for j in range(hp)]
    for ci in range(ts // c):
        sts = [chunk(j, ci, sts[j]) for j in range(hp)]
    for j in range(hp):
        state_ref[hb * hp + j] = sts[j]


def _hgrn(xb, w_in, lb_logits, gain, layer, ts, hp):
    bsz, seq, d = xb.shape
    dk = HGRN_DK
    nh = d // dk
    nhb = nh // hp
    ts = min(ts, seq)
    slots = lb_logits.shape[0]
    lbl = lb_logits.reshape(slots, nh, dk).transpose(1, 0, 2)
    buf = pltpu.VMEM((hp, ts, dk), F32)

    def wcols(part):
        return pl.BlockSpec((d, hp * dk), lambda b, s, h: (0, part * nhb + h))

    return pl.pallas_call(
        functools.partial(_hgrn_kernel, layer=layer, ts=ts, hp=hp),
        grid=(bsz, seq // ts, nhb),
        in_specs=[pl.BlockSpec((1, ts, d), lambda b, s, h: (b, s, 0)),
                  wcols(0), wcols(1), wcols(2), wcols(3),
                  pl.BlockSpec((hp, slots, dk), lambda b, s, h: (h, 0, 0)),
                  pl.BlockSpec((1, dk), lambda b, s, h: (0, 0))],
        out_specs=pl.BlockSpec((1, ts, hp * dk), lambda b, s, h: (b, s, h)),
        out_shape=jax.ShapeDtypeStruct((bsz, seq, d), MXU_DTYPE),
        scratch_shapes=[pltpu.VMEM((nh, dk, dk), F32), buf, buf, buf, buf, buf, buf],
        compiler_params=_params(("parallel", "arbitrary", "arbitrary")),
        name="hgrn2",
    )(xb, w_in, w_in, w_in, w_in, lbl, gain.reshape(1, dk))


def _attn_kernel(q_ref, kp_ref, kc_ref, vp_ref, vc_ref, bias_ref, sink_ref, o_ref, *, group, kvp):
    kvh = pl.program_id(0)
    n = pl.program_id(2)
    blk = ATTN_BLOCK
    si = lax.broadcasted_iota(jnp.int32, (1, 2 * blk), 1)
    first = jnp.where((si >= blk) | (n > 0), 0.0, NEG_INF)
    outs = []
    for j in range(kvp):
        kk = jnp.concatenate([kp_ref[0, j], kc_ref[0, j]], axis=0)
        vv = jnp.concatenate([vp_ref[0, j], vc_ref[0, j]], axis=0)
        for g in range(group):
            hq = j * group + g
            qh = q_ref[0, :, hq * HEAD_DIM:(hq + 1) * HEAD_DIM] * ATTN_SCALE
            s = _dot_nt(qh, kk) + bias_ref[hq] + first
            sink = sink_ref[kvh * kvp * group + hq]
            m = jnp.maximum(jnp.max(s, axis=-1, keepdims=True), sink)
            p = jnp.exp(s - m)
            w = p / (jnp.sum(p, axis=-1, keepdims=True) + jnp.exp(sink - m))
            outs.append(_dot(w.astype(MXU_DTYPE), vv))
    o_ref[0] = jnp.concatenate(outs, axis=-1).astype(o_ref.dtype)


def _attention(q, k_sh, v_sh, sinks, kvp):
    bsz, seq, dq = q.shape
    group = dq // (KV_HEADS * HEAD_DIM)
    blk = ATTN_BLOCK
    gw = kvp * group * HEAD_DIM
    cur = pl.BlockSpec((1, kvp, blk, HEAD_DIM), lambda k, b, n: (b, k, n, 0))
    prev = pl.BlockSpec((1, kvp, blk, HEAD_DIM), lambda k, b, n: (b, k, jnp.maximum(n - 1, 0), 0))
    nheads = KV_HEADS * group
    slopes = jnp.asarray([2.0 ** (-8.0 * h / nheads) for h in range(1, nheads + 1)], F32)
    dist = (jnp.arange(blk)[:, None] + blk - jnp.arange(2 * blk)[None, :])
    bias = jnp.where((dist >= 0) & (dist < blk),
                     -(slopes[:, None, None] * dist.astype(F32)[None]), NEG_INF)
    return pl.pallas_call(
        functools.partial(_attn_kernel, group=group, kvp=kvp),
        grid=(KV_HEADS // kvp, bsz, seq // blk),
        in_specs=[pl.BlockSpec((1, blk, gw), lambda k, b, n: (b, n, k)),
                  prev, cur, prev, cur,
                  pl.BlockSpec((kvp * group, blk, 2 * blk), lambda k, b, n: (k, 0, 0)),
                  pl.BlockSpec(memory_space=pltpu.SMEM)],
        out_specs=pl.BlockSpec((1, blk, gw), lambda k, b, n: (b, n, k)),
        out_shape=jax.ShapeDtypeStruct((bsz, seq, dq), MXU_DTYPE),
        compiler_params=_params(("parallel", "parallel", "arbitrary")),
        name="swa_attention",
    )(q, k_sh, k_sh, v_sh, v_sh, bias, sinks)


def _top16(s):
    rid = lax.broadcasted_iota(jnp.int32, (PEER_TOPK, s.shape[1]), 0)
    tops = jnp.zeros((PEER_TOPK, s.shape[1]), F32)
    work = s
    for kth in range(PEER_TOPK):
        m = jnp.max(work, axis=0, keepdims=True)
        tops = jnp.where(rid == kth, m, tops)
        work = jnp.where(work == m, -jnp.inf, work)
    return tops


def _peer_sel_kernel(x_ref, wq_ref, sub_ref, s1_ref, s2_ref, st_ref, *, tb, hps):
    qd = 2 * PEER_HALF
    qall = _dot(x_ref[...].astype(MXU_DTYPE), wq_ref[...]).astype(MXU_DTYPE)
    for j in range(hps):
        q = qall[:, j * qd:(j + 1) * qd]
        s1 = _dot_nt(sub_ref[j, 0], q[:, :PEER_HALF])
        s2 = _dot_nt(sub_ref[j, 1], q[:, PEER_HALF:])
        for lb in range(tb // LANES):
            a1 = s1[:, lb * LANES:(lb + 1) * LANES]
            a2 = s2[:, lb * LANES:(lb + 1) * LANES]
            s1_ref[lb, j] = a1
            s2_ref[lb, j] = a2
            t1 = _top16(a1)
            t2 = _top16(a2)
            cands = [t1[0:1] + t2]
            for a in range(1, SUBLANES):
                cands.append(t1[a:a + 1] + t2[0:SUBLANES])
            cands.append(t1[SUBLANES:] + t2[0:1])
            cand = jnp.concatenate(cands, axis=0)
            best = _top16(cand)
            top = best[0:1]
            tau = best[PEER_TOPK - 1:PEER_TOPK]
            z = jnp.sum(jnp.where(cand >= tau, jnp.exp(cand - top), 0.0), axis=0, keepdims=True)
            zero = jnp.zeros_like(z)
            st_ref[lb, j] = jnp.concatenate(
                [tau, t1[0:1], t2[0:1], 1.0 / z, zero, zero, zero, zero], axis=0)


def _peer_select(x, w_q, subkeys, tb, hps):
    t, d = x.shape
    tb = min(tb, t)
    nlb = tb // LANES
    qd = 2 * PEER_HALF
    blk = lambda rows: pl.BlockSpec((nlb, hps, rows, LANES), lambda i, h: (i, h, 0, 0))
    shp = lambda rows: jax.ShapeDtypeStruct((t // LANES, PEER_HEADS, rows, LANES), F32)
    return pl.pallas_call(
        functools.partial(_peer_sel_kernel, tb=tb, hps=hps),
        grid=(t // tb, PEER_HEADS // hps),
        in_specs=[pl.BlockSpec((tb, d), lambda i, h: (i, 0)),
                  pl.BlockSpec((d, hps * qd), lambda i, h: (0, h)),
                  pl.BlockSpec((hps, 2, PEER_NKEYS, PEER_HALF), lambda i, h: (h, 0, 0, 0))],
        out_specs=[blk(PEER_NKEYS), blk(PEER_NKEYS), blk(SUBLANES)],
        out_shape=[shp(PEER_NKEYS), shp(PEER_NKEYS), shp(SUBLANES)],
        compiler_params=_params(("parallel", "arbitrary")),
        name="peer_select",
    )(x, w_q, subkeys)


def _gelu(x):
    return 0.5 * x * (1.0 + lax.erf(x * (1.0 / math.sqrt(2.0))))


def _peer_dense_kernel(x_ref, u_ref, v_ref, s1_ref, s2_ref, st_ref, o_ref,
                       e2_ref, w1_ref, g_ref, at_ref, *, tb, ec):
    e = pl.program_id(1)
    nlb = tb // LANES
    nsub = ec // PEER_NKEYS

    @pl.when(e == 0)
    def _():
        o_ref[...] = jnp.zeros(o_ref.shape, F32)
        for lb in range(nlb):
            for h in range(PEER_HEADS):
                e2_ref[lb, h] = jnp.exp(s2_ref[lb, h] - st_ref[lb, h, 2:3, :])
                w1_ref[lb, h] = jnp.exp(s1_ref[lb, h] - st_ref[lb, h, 1:2, :]) * st_ref[lb, h, 3:4, :]

    for ii in range(nsub):
        i = e * nsub + ii
        for lb in range(nlb):
            lanes = slice(lb * LANES, (lb + 1) * LANES)
            for kt in range(PEER_NKEYS // GATE_ROWS):
                keys = slice(kt * GATE_ROWS, (kt + 1) * GATE_ROWS)
                gate = jnp.zeros((GATE_ROWS, LANES), F32)
                for h in range(PEER_HEADS):
                    tau = st_ref[lb, h, 0:1, :]
                    cs = s1_ref[lb, h, pl.ds(i, 1), :] + s2_ref[lb, h, keys, :]
                    val = e2_ref[lb, h, keys, :] * w1_ref[lb, h, pl.ds(i, 1), :]
                    gate = gate + jnp.where(cs >= tau, val, 0.0)
                g_ref[ii * PEER_NKEYS + kt * GATE_ROWS:ii * PEER_NKEYS + (kt + 1) * GATE_ROWS,
                      lanes] = gate

    ht = _dot_nt(u_ref[...], x_ref[...])
    at_ref[...] = (_gelu(ht) * g_ref[...]).astype(at_ref.dtype)
    o_ref[...] += _dot_tn(at_ref[...], v_ref[...])


def _peer_dense(x, u, v, s1, s2, st, tb, ec):
    t, d = x.shape
    tb = min(tb, t)
    nlb = tb // LANES
    ne = u.shape[0] // ec
    once = pl.Buffered(1)
    sblk = lambda rows: pl.BlockSpec((nlb, PEER_HEADS, rows, LANES), lambda i, e: (i, 0, 0, 0),
                                     pipeline_mode=once)
    return pl.pallas_call(
        functools.partial(_peer_dense_kernel, tb=tb, ec=ec),
        grid=(t // tb, ne),
        in_specs=[pl.BlockSpec((tb, d), lambda i, e: (i, 0), pipeline_mode=once),
                  pl.BlockSpec((ec, d), lambda i, e: (e, 0)),
                  pl.BlockSpec((ec, d), lambda i, e: (e, 0)),
                  sblk(PEER_NKEYS), sblk(PEER_NKEYS), sblk(SUBLANES)],
        out_specs=pl.BlockSpec((tb, d), lambda i, e: (i, 0), pipeline_mode=once),
        out_shape=jax.ShapeDtypeStruct((t, d), F32),
        scratch_shapes=[pltpu.VMEM((nlb, PEER_HEADS, PEER_NKEYS, LANES), F32),
                        pltpu.VMEM((nlb, PEER_HEADS, PEER_NKEYS, LANES), F32),
                        pltpu.VMEM((ec, tb), F32),
                        pltpu.VMEM((ec, tb), MXU_DTYPE)],
        compiler_params=_params(("parallel", "arbitrary")),
        name="peer_dense",
    )(x, u, v, s1, s2, st)


def _peer_layer(x, xb, w_q, subkeys, u, v, gain, bias, with_copy):
    s1, s2, st = _peer_select(xb, w_q, subkeys, tb=512, hps=2)
    y = _peer_dense(xb, u, v, s1, s2, st, tb=512, ec=1024)
    return _res_ln(x, y, gain, bias, tm=256, with_copy=with_copy)


def kernel(x, hgrn_w_in, hgrn_lb_logits, hgrn_norm_gain, hgrn_w_out, kv_w, attn_w_q, attn_sinks,
           attn_w_out, peer_w_q, peer_subkeys, peer_u, peer_v, ln_gain, ln_bias):
    bsz, seq, d = x.shape
    t = bsz * seq
    cdt = MXU_DTYPE
    dk = HGRN_DK
    nh = d // dk

    o = _hgrn(x.astype(cdt), hgrn_w_in[0].astype(cdt), hgrn_lb_logits, hgrn_norm_gain[0],
              layer=0, ts=512, hp=4)
    xt = x.reshape(t, d)
    x1, x1b = _matmul_ln(o.reshape(t, d), hgrn_w_out[0].astype(cdt), xt, ln_gain[0, 0], ln_bias[0, 0],
                         tm=512, tn=1024)
    x2, x2b = _peer_layer(x1, x1b, peer_w_q[0].astype(cdt), peer_subkeys[0].astype(cdt),
                          peer_u[0].astype(cdt), peer_v[0].astype(cdt), ln_gain[0, 1], ln_bias[0, 1],
                          with_copy=True)

    k_sh, v_sh = _kv_proj(x2b.reshape(bsz, seq, d), kv_w.astype(cdt), tm=512)

    q = _matmul(x2b, attn_w_q[0].astype(cdt), cdt, tm=512, tn=1024)
    att = _attention(q.reshape(bsz, seq, -1), k_sh, v_sh, attn_sinks[0].astype(F32), kvp=4)
    x3, x3b = _matmul_ln(att.reshape(t, -1), attn_w_out[0].astype(cdt), x2, ln_gain[1, 0], ln_bias[1, 0],
                         tm=512, tn=1024)
    (x4,) = _peer_layer(x3, x3b, peer_w_q[1].astype(cdt), peer_subkeys[1].astype(cdt),
                        peer_u[1].astype(cdt), peer_v[1].astype(cdt), ln_gain[1, 1], ln_bias[1, 1],
                        with_copy=False)
    return x4.reshape(bsz, seq, d)
```

```python
import functools
import math

import jax
import jax.numpy as jnp
from jax import lax
from jax.experimental import pallas as pl
from jax.experimental.pallas import tpu as pltpu

F32 = jnp.float32
MXU_DTYPE = jnp.bfloat16

DEPTH = 2
ALPHA = (2.0 * DEPTH) ** 0.25
LN_EPS = 1e-5
LN_UNROLL = 4
RMS_EPS = 1e-6

HGRN_DK = 128
HGRN_CHUNK = 32

HEAD_DIM = 64
KV_HEADS = 8
ATTN_BLOCK = 128
ATTN_SCALE = HEAD_DIM ** -0.5
NEG_INF = -1e30

PEER_HEADS = 8
PEER_NKEYS = 128
PEER_TOPK = 16
PEER_HALF = 128
GATE_ROWS = 32

LANES = 128
SUBLANES = 8
VMEM_LIMIT = 60 * 1024 * 1024


def _params(sem, flags=None):
    return pltpu.CompilerParams(dimension_semantics=sem, vmem_limit_bytes=VMEM_LIMIT, flags=flags)


def _sigmoid(x):
    return 1.0 / (1.0 + jnp.exp(-x))


def _dot(a, b):
    return jnp.dot(a, b, preferred_element_type=F32)


def _dot_nt(a, b):
    return lax.dot_general(a, b, (((1,), (1,)), ((), ())), preferred_element_type=F32)


def _dot_tn(a, b):
    return lax.dot_general(a, b, (((0,), (0,)), ((), ())), preferred_element_type=F32)


def _mm_kernel(a_ref, w_ref, o_ref):
    a = a_ref[...].astype(MXU_DTYPE)
    o_ref[...] = _dot(a, w_ref[...]).astype(o_ref.dtype)


def _matmul(a, w, out_dtype, tm, tn):
    m, k = a.shape
    n = w.shape[1]
    tm, tn = min(tm, m), min(tn, n)
    return pl.pallas_call(
        _mm_kernel,
        grid=(m // tm, n // tn),
        in_specs=[pl.BlockSpec((tm, k), lambda i, j: (i, 0)),
                  pl.BlockSpec((k, tn), lambda i, j: (0, j))],
        out_specs=pl.BlockSpec((tm, tn), lambda i, j: (i, j)),
        out_shape=jax.ShapeDtypeStruct((m, n), out_dtype),
        compiler_params=_params(("parallel", "arbitrary")),
        name="matmul",
    )(a, w)


def _kv_kernel(a_ref, w_ref, k_ref, v_ref):
    y = _dot(a_ref[0].astype(MXU_DTYPE), w_ref[...])
    nkv = KV_HEADS * HEAD_DIM
    for h in range(KV_HEADS):
        cols = slice(h * HEAD_DIM, (h + 1) * HEAD_DIM)
        k_ref[0, h] = y[:, cols].astype(k_ref.dtype)
        v_ref[0, h] = y[:, nkv:][:, cols].astype(v_ref.dtype)


def _kv_proj(x, w, tm):
    bsz, seq, d = x.shape
    tm = min(tm, seq)
    out = pl.BlockSpec((1, KV_HEADS, tm, HEAD_DIM), lambda b, i: (b, 0, i, 0))
    shape = jax.ShapeDtypeStruct((bsz, KV_HEADS, seq, HEAD_DIM), MXU_DTYPE)
    return pl.pallas_call(
        _kv_kernel,
        grid=(bsz, seq // tm),
        in_specs=[pl.BlockSpec((1, tm, d), lambda b, i: (b, i, 0)),
                  pl.BlockSpec(w.shape, lambda b, i: (0, 0))],
        out_specs=[out, out],
        out_shape=[shape, shape],
        compiler_params=_params(("parallel", "arbitrary")),
        name="kv_proj",
    )(x, w)


def _row_mean(z):
    n = z.shape[-1]
    return jnp.sum(_tree_sum(_lane_groups(z)), axis=-1, keepdims=True) * (1.0 / n)


def _lane_groups(z):
    return [z[:, k * LANES:(k + 1) * LANES] for k in range(z.shape[-1] // LANES)]


def _tree_sum(parts):
    while len(parts) > 1:
        parts = [a + b for a, b in zip(parts[0::2], parts[1::2])] + parts[len(parts) & ~1:]
    return parts[0]


def _ln_rows(z, g, b):
    zc = z - _row_mean(z)
    var = _row_mean(zc * zc)
    return zc * lax.rsqrt(var + LN_EPS) * g + b


def _mm_ln_kernel(a_ref, w_ref, res_ref, g_ref, b_ref, o_ref, ob_ref, mu_ref, rs_ref,
                  *, tn, nj, rows):
    j = pl.program_id(1)
    y = _dot(a_ref[...].astype(MXU_DTYPE), w_ref[...]) + ALPHA * res_ref[...]
    for jj in range(nj):
        @pl.when(j == jj)
        def _(jj=jj):
            o_ref[:, jj * tn:(jj + 1) * tn] = y

    @pl.when(j == nj - 1)
    def _():
        ngroups = o_ref.shape[0] // rows
        n = o_ref.shape[1]
        group = lambda r: pl.ds(pl.multiple_of(r * rows, rows), rows)

        def mean_pass(r, carry):
            sl = group(r)
            mu_ref[sl, :] = jnp.broadcast_to(_row_mean(o_ref[sl, :]), (rows, LANES))
            return carry

        def var_pass(r, carry):
            sl = group(r)
            mu = mu_ref[sl, :]
            sq = [(zk - mu) * (zk - mu) for zk in _lane_groups(o_ref[sl, :])]
            var = jnp.sum(_tree_sum(sq), axis=-1, keepdims=True) * (1.0 / n)
            rs_ref[sl, :] = jnp.broadcast_to(lax.rsqrt(var + LN_EPS), (rows, LANES))
            return carry

        def norm_pass(r, carry):
            sl = group(r)
            mu = mu_ref[sl, :]
            rs = rs_ref[sl, :]
            for k, zk in enumerate(_lane_groups(o_ref[sl, :])):
                cols = slice(k * LANES, (k + 1) * LANES)
                out = (zk - mu) * rs * g_ref[:, cols] + b_ref[:, cols]
                o_ref[sl, cols] = out
                ob_ref[sl, cols] = out.astype(ob_ref.dtype)
            return carry

        lax.fori_loop(0, ngroups, mean_pass, 0, unroll=2 * LN_UNROLL)
        lax.fori_loop(0, ngroups, var_pass, 0, unroll=2 * LN_UNROLL)
        lax.fori_loop(0, ngroups, norm_pass, 0, unroll=LN_UNROLL)


def _matmul_ln(a, w, res, gain, bias, tm, tn):
    m, k = a.shape
    n = w.shape[1]
    tm, tn = min(tm, m), min(tn, n)
    nj = n // tn
    full = pl.BlockSpec((tm, n), lambda i, j: (i, 0))
    return pl.pallas_call(
        functools.partial(_mm_ln_kernel, tn=tn, nj=nj, rows=SUBLANES),
        grid=(m // tm, nj),
        in_specs=[pl.BlockSpec((tm, k), lambda i, j: (i, 0)),
                  pl.BlockSpec((k, tn), lambda i, j: (0, j)),
                  pl.BlockSpec((tm, tn), lambda i, j: (i, j)),
                  pl.BlockSpec((1, n), lambda i, j: (0, 0)),
                  pl.BlockSpec((1, n), lambda i, j: (0, 0))],
        out_specs=[full, full],
        out_shape=[jax.ShapeDtypeStruct((m, n), F32), jax.ShapeDtypeStruct((m, n), MXU_DTYPE)],
        scratch_shapes=[pltpu.VMEM((tm, LANES), F32), pltpu.VMEM((tm, LANES), F32)],
        compiler_params=_params(("parallel", "arbitrary")),
        name="matmul_ln",
    )(a, w, res, gain.reshape(1, n), bias.reshape(1, n))


def _res_ln_kernel(x_ref, y_ref, g_ref, b_ref, *o_refs):
    z = _ln_rows(ALPHA * x_ref[...] + y_ref[...], g_ref[...], b_ref[...])
    for o_ref in o_refs:
        o_ref[...] = z.astype(o_ref.dtype)


def _res_ln(x, y, gain, bias, tm, with_copy):
    m, n = x.shape
    tm = min(tm, m)
    row = pl.BlockSpec((tm, n), lambda i: (i, 0))
    vec = pl.BlockSpec((1, n), lambda i: (0, 0))
    dtypes = (F32, MXU_DTYPE) if with_copy else (F32,)
    return pl.pallas_call(
        _res_ln_kernel,
        grid=(m // tm,),
        in_specs=[row, row, vec, vec],
        out_specs=[row] * len(dtypes),
        out_shape=[jax.ShapeDtypeStruct((m, n), dt) for dt in dtypes],
        compiler_params=_params(("parallel",)),
        name="res_ln",
    )(x, y, gain.reshape(1, n), bias.reshape(1, n))


def _hgrn_kernel(x_ref, wq_ref, wf_ref, wi_ref, wg_ref, lbl_ref, gain_ref, o_ref,
                 state_ref, qg_ref, kg_ref, k_ref, b_ref, v_ref, g_ref, *, layer, ts, hp):
    sblk = pl.program_id(1)
    hb = pl.program_id(2)
    dk = HGRN_DK
    c = HGRN_CHUNK

    @pl.when(sblk == 0)
    def _():
        for j in range(hp):
            state_ref[hb * hp + j] = jnp.zeros((dk, dk), F32)

    x = x_ref[0]
    pq, pf, pi, pg = (_dot(x, w[...]) for w in (wq_ref, wf_ref, wi_ref, wg_ref))
    row = lax.broadcasted_iota(jnp.int32, (ts, dk), 0) % c
    for j in range(hp):
        cols = slice(j * dk, (j + 1) * dk)
        lg = lbl_ref[j]
        ex = jnp.exp(lg - jnp.max(lg, axis=0, keepdims=True))
        lb = jnp.sum(ex[:layer + 1], axis=0, keepdims=True) / jnp.sum(ex, axis=0, keepdims=True)

        q_raw = pq[:, cols]
        f_raw = pf[:, cols]
        f = lb + (1.0 - lb) * _sigmoid(f_raw)
        k = 1.0 - f
        b = jnp.log(f)
        sh = 1
        while sh < c:
            b = b + jnp.where(row >= sh, pltpu.roll(b, sh, axis=0), 0.0)
            sh *= 2
        qg_ref[j] = q_raw * _sigmoid(q_raw) * jnp.exp(b)
        kg_ref[j] = k * jnp.exp(-b)
        k_ref[j] = k
        b_ref[j] = b
        v_ref[j] = pi[:, cols]
        g_ref[j] = pg[:, cols]

    causal = (lax.broadcasted_iota(jnp.int32, (c, c), 0)
              >= lax.broadcasted_iota(jnp.int32, (c, c), 1))
    gain = gain_ref[...]

    def chunk(j, ci, st):
        sl = slice(ci * c, (ci + 1) * c)
        qg = qg_ref[j, sl, :].astype(MXU_DTYPE)
        vv = v_ref[j, sl, :].astype(MXU_DTYPE)
        bc = b_ref[j, sl, :]
        bl = bc[c - 1:c, :]
        a = jnp.where(causal, _dot_nt(qg, kg_ref[j, sl, :].astype(MXU_DTYPE)), 0.0)
        o = _dot(a.astype(MXU_DTYPE), vv) + _dot_nt(qg, st.astype(MXU_DTYPE))
        kd = (k_ref[j, sl, :] * jnp.exp(bl - bc)).astype(MXU_DTYPE)
        st = st * jnp.exp(bl) + _dot_tn(vv, kd)
        o = o * lax.rsqrt(jnp.mean(o * o, axis=-1, keepdims=True) + RMS_EPS) * gain
        gr = g_ref[j, sl, :]
        o_ref[0, sl, j * dk:(j + 1) * dk] = (o * (gr * _sigmoid(gr))).astype(o_ref.dtype)
        return st

    sts = [state_ref[hb * hp + j] for j in range(hp)]
    for ci in range(ts // c):
        sts = [chunk(j, ci, sts[j]) for j in range(hp)]
    for j in range(hp):
        state_ref[hb * hp + j] = sts[j]


def _hgrn(xb, w_in, lb_logits, gain, layer, ts, hp):
    bsz, seq, d = xb.shape
    dk = HGRN_DK
    nh = d // dk
    nhb = nh // hp
    ts = min(ts, seq)
    slots = lb_logits.shape[0]
    lbl = lb_logits.reshape(slots, nh, dk).transpose(1, 0, 2)
    buf = pltpu.VMEM((hp, ts, dk), F32)

    def wcols(part):
        return pl.BlockSpec((d, hp * dk), lambda b, s, h: (0, part * nhb + h))

    return pl.pallas_call(
        functools.partial(_hgrn_kernel, layer=layer, ts=ts, hp=hp),
        grid=(bsz, seq // ts, nhb),
        in_specs=[pl.BlockSpec((1, ts, d), lambda b, s, h: (b, s, 0)),
                  wcols(0), wcols(1), wcols(2), wcols(3),
                  pl.BlockSpec((hp, slots, dk), lambda b, s, h: (h, 0, 0)),
                  pl.BlockSpec((1, dk), lambda b, s, h: (0, 0))],
        out_specs=pl.BlockSpec((1, ts, hp * dk), lambda b, s, h: (b, s, h)),
        out_shape=jax.ShapeDtypeStruct((bsz, seq, d), MXU_DTYPE),
        scratch_shapes=[pltpu.VMEM((nh, dk, dk), F32), buf, buf, buf, buf, buf, buf],
        compiler_params=_params(("parallel", "arbitrary", "arbitrary")),
        name="hgrn2",
    )(xb, w_in, w_in, w_in, w_in, lbl, gain.reshape(1, dk))


def _attn_kernel(q_ref, kp_ref, kc_ref, vp_ref, vc_ref, bias_ref, sink_ref, o_ref, *, group, kvp):
    kvh = pl.program_id(0)
    n = pl.program_id(2)
    blk = ATTN_BLOCK
    si = lax.broadcasted_iota(jnp.int32, (1, 2 * blk), 1)
    first = jnp.where((si >= blk) | (n > 0), 0.0, NEG_INF)
    outs = []
    for j in range(kvp):
        kk = jnp.concatenate([kp_ref[0, j], kc_ref[0, j]], axis=0)
        vv = jnp.concatenate([vp_ref[0, j], vc_ref[0, j]], axis=0)
        for g in range(group):
            hq = j * group + g
            qh = q_ref[0, :, hq * HEAD_DIM:(hq + 1) * HEAD_DIM] * ATTN_SCALE
            s = _dot_nt(qh, kk) + bias_ref[hq] + first
            sink = sink_ref[kvh * kvp * group + hq]
            m = jnp.maximum(jnp.max(s, axis=-1, keepdims=True), sink)
            p = jnp.exp(s - m)
            w = p / (jnp.sum(p, axis=-1, keepdims=True) + jnp.exp(sink - m))
            outs.append(_dot(w.astype(MXU_DTYPE), vv))
    o_ref[0] = jnp.concatenate(outs, axis=-1).astype(o_ref.dtype)


def _attention(q, k_sh, v_sh, sinks, kvp):
    bsz, seq, dq = q.shape
    group = dq // (KV_HEADS * HEAD_DIM)
    blk = ATTN_BLOCK
    gw = kvp * group * HEAD_DIM
    cur = pl.BlockSpec((1, kvp, blk, HEAD_DIM), lambda k, b, n: (b, k, n, 0))
    prev = pl.BlockSpec((1, kvp, blk, HEAD_DIM), lambda k, b, n: (b, k, jnp.maximum(n - 1, 0), 0))
    nheads = KV_HEADS * group
    slopes = jnp.asarray([2.0 ** (-8.0 * h / nheads) for h in range(1, nheads + 1)], F32)
    dist = (jnp.arange(blk)[:, None] + blk - jnp.arange(2 * blk)[None, :])
    bias = jnp.where((dist >= 0) & (dist < blk),
                     -(slopes[:, None, None] * dist.astype(F32)[None]), NEG_INF)
    return pl.pallas_call(
        functools.partial(_attn_kernel, group=group, kvp=kvp),
        grid=(KV_HEADS // kvp, bsz, seq // blk),
        in_specs=[pl.BlockSpec((1, blk, gw), lambda k, b, n: (b, n, k)),
                  prev, cur, prev, cur,
                  pl.BlockSpec((kvp * group, blk, 2 * blk), lambda k, b, n: (k, 0, 0)),
                  pl.BlockSpec(memory_space=pltpu.SMEM)],
        out_specs=pl.BlockSpec((1, blk, gw), lambda k, b, n: (b, n, k)),
        out_shape=jax.ShapeDtypeStruct((bsz, seq, dq), MXU_DTYPE),
        compiler_params=_params(("parallel", "parallel", "arbitrary")),
        name="swa_attention",
    )(q, k_sh, k_sh, v_sh, v_sh, bias, sinks)


def _top16(s):
    rid = lax.broadcasted_iota(jnp.int32, (PEER_TOPK, s.shape[1]), 0)
    tops = jnp.zeros((PEER_TOPK, s.shape[1]), F32)
    work = s
    for kth in range(PEER_TOPK):
        m = jnp.max(work, axis=0, keepdims=True)
        tops = jnp.where(rid == kth, m, tops)
        work = jnp.where(work == m, -jnp.inf, work)
    return tops


def _peer_sel_kernel(x_ref, wq_ref, sub_ref, s1_ref, s2_ref, st_ref, *, tb, hps):
    qd = 2 * PEER_HALF
    qall = _dot(x_ref[...].astype(MXU_DTYPE), wq_ref[...]).astype(MXU_DTYPE)
    for j in range(hps):
        q = qall[:, j * qd:(j + 1) * qd]
        s1 = _dot_nt(sub_ref[j, 0], q[:, :PEER_HALF])
        s2 = _dot_nt(sub_ref[j, 1], q[:, PEER_HALF:])
        for lb in range(tb // LANES):
            a1 = s1[:, lb * LANES:(lb + 1) * LANES]
            a2 = s2[:, lb * LANES:(lb + 1) * LANES]
            s1_ref[lb, j] = a1
            s2_ref[lb, j] = a2
            t1 = _top16(a1)
            t2 = _top16(a2)
            cands = [t1[0:1] + t2]
            for a in range(1, SUBLANES):
                cands.append(t1[a:a + 1] + t2[0:SUBLANES])
            cands.append(t1[SUBLANES:] + t2[0:1])
            cand = jnp.concatenate(cands, axis=0)
            best = _top16(cand)
            top = best[0:1]
            tau = best[PEER_TOPK - 1:PEER_TOPK]
            z = jnp.sum(jnp.where(cand >= tau, jnp.exp(cand - top), 0.0), axis=0, keepdims=True)
            zero = jnp.zeros_like(z)
            st_ref[lb, j] = jnp.concatenate(
                [tau, t1[0:1], t2[0:1], 1.0 / z, zero, zero, zero, zero], axis=0)


def _peer_select(x, w_q, subkeys, tb, hps):
    t, d = x.shape
    tb = min(tb, t)
    nlb = tb // LANES
    qd = 2 * PEER_HALF
    blk = lambda rows: pl.BlockSpec((nlb, hps, rows, LANES), lambda i, h: (i, h, 0, 0))
    shp = lambda rows: jax.ShapeDtypeStruct((t // LANES, PEER_HEADS, rows, LANES), F32)
    return pl.pallas_call(
        functools.partial(_peer_sel_kernel, tb=tb, hps=hps),
        grid=(t // tb, PEER_HEADS // hps),
        in_specs=[pl.BlockSpec((tb, d), lambda i, h: (i, 0)),
                  pl.BlockSpec((d, hps * qd), lambda i, h: (0, h)),
                  pl.BlockSpec((hps, 2, PEER_NKEYS, PEER_HALF), lambda i, h: (h, 0, 0, 0))],
        out_specs=[blk(PEER_NKEYS), blk(PEER_NKEYS), blk(SUBLANES)],
        out_shape=[shp(PEER_NKEYS), shp(PEER_NKEYS), shp(SUBLANES)],
        compiler_params=_params(("parallel", "arbitrary")),
        name="peer_select",
    )(x, w_q, subkeys)


def _gelu(x):
    return 0.5 * x * (1.0 + lax.erf(x * (1.0 / math.sqrt(2.0))))


def _peer_dense_kernel(x_ref, u_ref, v_ref, s1_ref, s2_ref, st_ref, o_ref,
                       e2_ref, w1_ref, g_ref, at_ref, *, tb, ec):
    e = pl.program_id(1)
    nlb = tb // LANES
    nsub = ec // PEER_NKEYS

    @pl.when(e == 0)
    def _():
        o_ref[...] = jnp.zeros(o_ref.shape, F32)
        for lb in range(nlb):
            for h in range(PEER_HEADS):
                e2_ref[lb, h] = jnp.exp(s2_ref[lb, h] - st_ref[lb, h, 2:3, :])
                w1_ref[lb, h] = jnp.exp(s1_ref[lb, h] - st_ref[lb, h, 1:2, :]) * st_ref[lb, h, 3:4, :]

    for ii in range(nsub):
        i = e * nsub + ii
        for lb in range(nlb):
            lanes = slice(lb * LANES, (lb + 1) * LANES)
            for kt in range(PEER_NKEYS // GATE_ROWS):
                keys = slice(kt * GATE_ROWS, (kt + 1) * GATE_ROWS)
                gate = jnp.zeros((GATE_ROWS, LANES), F32)
                for h in range(PEER_HEADS):
                    tau = st_ref[lb, h, 0:1, :]
                    cs = s1_ref[lb, h, pl.ds(i, 1), :] + s2_ref[lb, h, keys, :]
                    val = e2_ref[lb, h, keys, :] * w1_ref[lb, h, pl.ds(i, 1), :]
                    gate = gate + jnp.where(cs >= tau, val, 0.0)
                g_ref[ii * PEER_NKEYS + kt * GATE_ROWS:ii * PEER_NKEYS + (kt + 1) * GATE_ROWS,
                      lanes] = gate

    ht = _dot_nt(u_ref[...], x_ref[...])
    at_ref[...] = (_gelu(ht) * g_ref[...]).astype(at_ref.dtype)
    o_ref[...] += _dot_tn(at_ref[...], v_ref[...])


def _peer_dense(x, u, v, s1, s2, st, tb, ec):
    t, d = x.shape
    tb = min(tb, t)
    nlb = tb // LANES
    ne = u.shape[0] // ec
    once = pl.Buffered(1)
    sblk = lambda rows: pl.BlockSpec((nlb, PEER_HEADS, rows, LANES), lambda i, e: (i, 0, 0, 0),
                                     pipeline_mode=once)
    return pl.pallas_call(
        functools.partial(_peer_dense_kernel, tb=tb, ec=ec),
        grid=(t // tb, ne),
        in_specs=[pl.BlockSpec((tb, d), lambda i, e: (i, 0), pipeline_mode=once),
                  pl.BlockSpec((ec, d), lambda i, e: (e, 0)),
                  pl.BlockSpec((ec, d), lambda i, e: (e, 0)),
                  sblk(PEER_NKEYS), sblk(PEER_NKEYS), sblk(SUBLANES)],
        out_specs=pl.BlockSpec((tb, d), lambda i, e: (i, 0), pipeline_mode=once),
        out_shape=jax.ShapeDtypeStruct((t, d), F32),
        scratch_shapes=[pltpu.VMEM((nlb, PEER_HEADS, PEER_NKEYS, LANES), F32),
                        pltpu.VMEM((nlb, PEER_HEADS, PEER_NKEYS, LANES), F32),
                        pltpu.VMEM((ec, tb), F32),
                        pltpu.VMEM((ec, tb), MXU_DTYPE)],
        compiler_params=_params(("parallel", "arbitrary")),
        name="peer_dense",
    )(x, u, v, s1, s2, st)


def _peer_layer(x, xb, w_q, subkeys, u, v, gain, bias, with_copy):
    s1, s2, st = _peer_select(xb, w_q, subkeys, tb=512, hps=2)
    y = _peer_dense(xb, u, v, s1, s2, st, tb=512, ec=1024)
    return _res_ln(x, y, gain, bias, tm=256, with_copy=with_copy)


def kernel(x, hgrn_w_in, hgrn_lb_logits, hgrn_norm_gain, hgrn_w_out, kv_w, attn_w_q, attn_sinks,
           attn_w_out, peer_w_q, peer_subkeys, peer_u, peer_v, ln_gain, ln_bias):
    bsz, seq, d = x.shape
    t = bsz * seq
    cdt = MXU_DTYPE
    dk = HGRN_DK
    nh = d // dk

    o = _hgrn(x.astype(cdt), hgrn_w_in[0].astype(cdt), hgrn_lb_logits, hgrn_norm_gain[0],
              layer=0, ts=512, hp=4)
    xt = x.reshape(t, d)
    x1, x1b = _matmul_ln(o.reshape(t, d), hgrn_w_out[0].astype(cdt), xt, ln_gain[0, 0], ln_bias[0, 0],
                         tm=512, tn=1024)
    x2, x2b = _peer_layer(x1, x1b, peer_w_q[0].astype(cdt), peer_subkeys[0].astype(cdt),
                          peer_u[0].astype(cdt), peer_v[0].astype(cdt), ln_gain[0, 1], ln_bias[0, 1],
                          with_copy=True)

    k_sh, v_sh = _kv_proj(x2b.reshape(bsz, seq, d), kv_w.astype(cdt), tm=512)

    q = _matmul(x2b, attn_w_q[0].astype(cdt), cdt, tm=512, tn=1024)
    att = _attention(q.reshape(bsz, seq, -1), k_sh, v_sh, attn_sinks[0].astype(F32), kvp=4)
    x3, x3b = _matmul_ln(att.reshape(t, -1), attn_w_out[0].astype(cdt), x2, ln_gain[1, 0], ln_bias[1, 0],
                         tm=512, tn=1024)
    (x4,) = _peer_layer(x3, x3b, peer_w_q[1].astype(cdt), peer_subkeys[1].astype(cdt),
                        peer_u[1].astype(cdt), peer_v[1].astype(cdt), ln_gain[1, 1], ln_bias[1, 1],
                        with_copy=False)
    return x4.reshape(bsz, seq, d)
```

```python
import functools
import math

import jax
import jax.numpy as jnp
from jax import lax
from jax.experimental import pallas as pl
from jax.experimental.pallas import tpu as pltpu

F32 = jnp.float32
MXU_DTYPE = jnp.bfloat16

DEPTH = 2
ALPHA = (2.0 * DEPTH) ** 0.25
LN_EPS = 1e-5
LN_UNROLL = 4
RMS_EPS = 1e-6

HGRN_DK = 128
HGRN_CHUNK = 32

HEAD_DIM = 64
KV_HEADS = 8
ATTN_BLOCK = 128
ATTN_SCALE = HEAD_DIM ** -0.5
NEG_INF = -1e30

PEER_HEADS = 8
PEER_NKEYS = 128
PEER_TOPK = 16
PEER_HALF = 128
GATE_ROWS = 32

LANES = 128
SUBLANES = 8
VMEM_LIMIT = 60 * 1024 * 1024


def _params(sem, flags=None):
    return pltpu.CompilerParams(dimension_semantics=sem, vmem_limit_bytes=VMEM_LIMIT, flags=flags)


def _sigmoid(x):
    return 1.0 / (1.0 + jnp.exp(-x))


def _dot(a, b):
    return jnp.dot(a, b, preferred_element_type=F32)


def _dot_nt(a, b):
    return lax.dot_general(a, b, (((1,), (1,)), ((), ())), preferred_element_type=F32)


def _dot_tn(a, b):
    return lax.dot_general(a, b, (((0,), (0,)), ((), ())), preferred_element_type=F32)


def _mm_kernel(a_ref, w_ref, o_ref):
    a = a_ref[...].astype(MXU_DTYPE)
    o_ref[...] = _dot(a, w_ref[...]).astype(o_ref.dtype)


def _matmul(a, w, out_dtype, tm, tn):
    m, k = a.shape
    n = w.shape[1]
    tm, tn = min(tm, m), min(tn, n)
    return pl.pallas_call(
        _mm_kernel,
        grid=(m // tm, n // tn),
        in_specs=[pl.BlockSpec((tm, k), lambda i, j: (i, 0)),
                  pl.BlockSpec((k, tn), lambda i, j: (0, j))],
        out_specs=pl.BlockSpec((tm, tn), lambda i, j: (i, j)),
        out_shape=jax.ShapeDtypeStruct((m, n), out_dtype),
        compiler_params=_params(("parallel", "arbitrary")),
        name="matmul",
    )(a, w)


def _kv_kernel(a_ref, w_ref, k_ref, v_ref):
    y = _dot(a_ref[0].astype(MXU_DTYPE), w_ref[...])
    nkv = KV_HEADS * HEAD_DIM
    for h in range(KV_HEADS):
        cols = slice(h * HEAD_DIM, (h + 1) * HEAD_DIM)
        k_ref[0, h] = y[:, cols].astype(k_ref.dtype)
        v_ref[0, h] = y[:, nkv:][:, cols].astype(v_ref.dtype)


def _kv_proj(x, w, tm):
    bsz, seq, d = x.shape
    tm = min(tm, seq)
    out = pl.BlockSpec((1, KV_HEADS, tm, HEAD_DIM), lambda b, i: (b, 0, i, 0))
    shape = jax.ShapeDtypeStruct((bsz, KV_HEADS, seq, HEAD_DIM), MXU_DTYPE)
    return pl.pallas_call(
        _kv_kernel,
        grid=(bsz, seq // tm),
        in_specs=[pl.BlockSpec((1, tm, d), lambda b, i: (b, i, 0)),
                  pl.BlockSpec(w.shape, lambda b, i: (0, 0))],
        out_specs=[out, out],
        out_shape=[shape, shape],
        compiler_params=_params(("parallel", "arbitrary")),
        name="kv_proj",
    )(x, w)


def _row_mean(z):
    n = z.shape[-1]
    return jnp.sum(_tree_sum(_lane_groups(z)), axis=-1, keepdims=True) * (1.0 / n)


def _lane_groups(z):
    return [z[:, k * LANES:(k + 1) * LANES] for k in range(z.shape[-1] // LANES)]


def _tree_sum(parts):
    while len(parts) > 1:
        parts = [a + b for a, b in zip(parts[0::2], parts[1::2])] + parts[len(parts) & ~1:]
    return parts[0]


def _ln_rows(z, g, b):
    zc = z - _row_mean(z)
    var = _row_mean(zc * zc)
    return zc * lax.rsqrt(var + LN_EPS) * g + b


def _mm_ln_kernel(a_ref, w_ref, res_ref, g_ref, b_ref, o_ref, ob_ref, mu_ref, rs_ref,
                  *, tn, nj, rows):
    j = pl.program_id(1)
    y = _dot(a_ref[...].astype(MXU_DTYPE), w_ref[...]) + ALPHA * res_ref[...]
    for jj in range(nj):
        @pl.when(j == jj)
        def _(jj=jj):
            o_ref[:, jj * tn:(jj + 1) * tn] = y

    @pl.when(j == nj - 1)
    def _():
        ngroups = o_ref.shape[0] // rows
        n = o_ref.shape[1]
        group = lambda r: pl.ds(pl.multiple_of(r * rows, rows), rows)

        def mean_pass(r, carry):
            sl = group(r)
            mu_ref[sl, :] = jnp.broadcast_to(_row_mean(o_ref[sl, :]), (rows, LANES))
            return carry

        def var_pass(r, carry):
            sl = group(r)
            mu = mu_ref[sl, :]
            sq = [(zk - mu) * (zk - mu) for zk in _lane_groups(o_ref[sl, :])]
            var = jnp.sum(_tree_sum(sq), axis=-1, keepdims=True) * (1.0 / n)
            rs_ref[sl, :] = jnp.broadcast_to(lax.rsqrt(var + LN_EPS), (rows, LANES))
            return carry

        def norm_pass(r, carry):
            sl = group(r)
            mu = mu_ref[sl, :]
            rs = rs_ref[sl, :]
            for k, zk in enumerate(_lane_groups(o_ref[sl, :])):
                cols = slice(k * LANES, (k + 1) * LANES)
                out = (zk - mu) * rs * g_ref[:, cols] + b_ref[:, cols]
                o_ref[sl, cols] = out
                ob_ref[sl, cols] = out.astype(ob_ref.dtype)
            return carry

        lax.fori_loop(0, ngroups, mean_pass, 0, unroll=2 * LN_UNROLL)
        lax.fori_loop(0, ngroups, var_pass, 0, unroll=2 * LN_UNROLL)
        lax.fori_loop(0, ngroups, norm_pass, 0, unroll=LN_UNROLL)


def _matmul_ln(a, w, res, gain, bias, tm, tn):
    m, k = a.shape
    n = w.shape[1]
    tm, tn = min(tm, m), min(tn, n)
    nj = n // tn
    full = pl.BlockSpec((tm, n), lambda i, j: (i, 0))
    return pl.pallas_call(
        functools.partial(_mm_ln_kernel, tn=tn, nj=nj, rows=SUBLANES),
        grid=(m // tm, nj),
        in_specs=[pl.BlockSpec((tm, k), lambda i, j: (i, 0)),
                  pl.BlockSpec((k, tn), lambda i, j: (0, j)),
                  pl.BlockSpec((tm, tn), lambda i, j: (i, j)),
                  pl.BlockSpec((1, n), lambda i, j: (0, 0)),
                  pl.BlockSpec((1, n), lambda i, j: (0, 0))],
        out_specs=[full, full],
        out_shape=[jax.ShapeDtypeStruct((m, n), F32), jax.ShapeDtypeStruct((m, n), MXU_DTYPE)],
        scratch_shapes=[pltpu.VMEM((tm, LANES), F32), pltpu.VMEM((tm, LANES), F32)],
        compiler_params=_params(("parallel", "arbitrary")),
        name="matmul_ln",
    )(a, w, res, gain.reshape(1, n), bias.reshape(1, n))


def _res_ln_kernel(x_ref, y_ref, g_ref, b_ref, *o_refs):
    z = _ln_rows(ALPHA * x_ref[...] + y_ref[...], g_ref[...], b_ref[...])
    for o_ref in o_refs:
        o_ref[...] = z.astype(o_ref.dtype)


def _res_ln(x, y, gain, bias, tm, with_copy):
    m, n = x.shape
    tm = min(tm, m)
    row = pl.BlockSpec((tm, n), lambda i: (i, 0))
    vec = pl.BlockSpec((1, n), lambda i: (0, 0))
    dtypes = (F32, MXU_DTYPE) if with_copy else (F32,)
    return pl.pallas_call(
        _res_ln_kernel,
        grid=(m // tm,),
        in_specs=[row, row, vec, vec],
        out_specs=[row] * len(dtypes),
        out_shape=[jax.ShapeDtypeStruct((m, n), dt) for dt in dtypes],
        compiler_params=_params(("parallel",)),
        name="res_ln",
    )(x, y, gain.reshape(1, n), bias.reshape(1, n))


def _hgrn_kernel(x_ref, wq_ref, wf_ref, wi_ref, wg_ref, lbl_ref, gain_ref, o_ref,
                 state_ref, qg_ref, kg_ref, k_ref, b_ref, v_ref, g_ref, *, layer, ts, hp):
    sblk = pl.program_id(1)
    hb = pl.program_id(2)
    dk = HGRN_DK
    c = HGRN_CHUNK

    @pl.when(sblk == 0)
    def _():
        for j in range(hp):
            state_ref[hb * hp + j] = jnp.zeros((dk, dk), F32)

    x = x_ref[0]
    pq, pf, pi, pg = (_dot(x, w[...]) for w in (wq_ref, wf_ref, wi_ref, wg_ref))
    row = lax.broadcasted_iota(jnp.int32, (ts, dk), 0) % c
    for j in range(hp):
        cols = slice(j * dk, (j + 1) * dk)
        lg = lbl_ref[j]
        ex = jnp.exp(lg - jnp.max(lg, axis=0, keepdims=True))
        lb = jnp.sum(ex[:layer + 1], axis=0, keepdims=True) / jnp.sum(ex, axis=0, keepdims=True)

        q_raw = pq[:, cols]
        f_raw = pf[:, cols]
        f = lb + (1.0 - lb) * _sigmoid(f_raw)
        k = 1.0 - f
        b = jnp.log(f)
        sh = 1
        while sh < c:
            b = b + jnp.where(row >= sh, pltpu.roll(b, sh, axis=0), 0.0)
            sh *= 2
        qg_ref[j] = q_raw * _sigmoid(q_raw) * jnp.exp(b)
        kg_ref[j] = k * jnp.exp(-b)
        k_ref[j] = k
        b_ref[j] = b
        v_ref[j] = pi[:, cols]
        g_ref[j] = pg[:, cols]

    causal = (lax.broadcasted_iota(jnp.int32, (c, c), 0)
              >= lax.broadcasted_iota(jnp.int32, (c, c), 1))
    gain = gain_ref[...]

    def chunk(j, ci, st):
        sl = slice(ci * c, (ci + 1) * c)
        qg = qg_ref[j, sl, :].astype(MXU_DTYPE)
        vv = v_ref[j, sl, :].astype(MXU_DTYPE)
        bc = b_ref[j, sl, :]
        bl = bc[c - 1:c, :]
        a = jnp.where(causal, _dot_nt(qg, kg_ref[j, sl, :].astype(MXU_DTYPE)), 0.0)
        o = _dot(a.astype(MXU_DTYPE), vv) + _dot_nt(qg, st.astype(MXU_DTYPE))
        kd = (k_ref[j, sl, :] * jnp.exp(bl - bc)).astype(MXU_DTYPE)
        st = st * jnp.exp(bl) + _dot_tn(vv, kd)
        o = o * lax.rsqrt(jnp.mean(o * o, axis=-1, keepdims=True) + RMS_EPS) * gain
        gr = g_ref[j, sl, :]
        o_ref[0, sl, j * dk:(j + 1) * dk] = (o * (gr * _sigmoid(gr))).astype(o_ref.dtype)
        return st

    sts = [state_ref[hb * hp + j] for j in range(hp)]
    for ci in range(ts // c):
        sts = [chunk(j, ci, sts[j]) for j in range(hp)]
    for j in range(hp):
        state_ref[hb * hp + j] = sts[j]


def _hgrn(xb, w_in, lb_logits, gain, layer, ts, hp):
    bsz, seq, d = xb.shape
    dk = HGRN_DK
    nh = d // dk
    nhb = nh // hp
    ts = min(ts, seq)
    slots = lb_logits.shape[0]
    lbl = lb_logits.reshape(slots, nh, dk).transpose(1, 0, 2)
    buf = pltpu.VMEM((hp, ts, dk), F32)

    def wcols(part):
        return pl.BlockSpec((d, hp * dk), lambda b, s, h: (0, part * nhb + h))

    return pl.pallas_call(
        functools.partial(_hgrn_kernel, layer=layer, ts=ts, hp=hp),
        grid=(bsz, seq // ts, nhb),
        in_specs=[pl.BlockSpec((1, ts, d), lambda b, s, h: (b, s, 0)),
                  wcols(0), wcols(1), wcols(2), wcols(3),
                  pl.BlockSpec((hp, slots, dk), lambda b, s, h: (h, 0, 0)),
                  pl.BlockSpec((1, dk), lambda b, s, h: (0, 0))],
        out_specs=pl.BlockSpec((1, ts, hp * dk), lambda b, s, h: (b, s, h)),
        out_shape=jax.ShapeDtypeStruct((bsz, seq, d), MXU_DTYPE),
        scratch_shapes=[pltpu.VMEM((nh, dk, dk), F32), buf, buf, buf, buf, buf, buf],
        compiler_params=_params(("parallel", "arbitrary", "arbitrary")),
        name="hgrn2",
    )(xb, w_in, w_in, w_in, w_in, lbl, gain.reshape(1, dk))


def _attn_kernel(q_ref, kp_ref, kc_ref, vp_ref, vc_ref, bias_ref, sink_ref, o_ref, *, group, kvp):
    kvh = pl.program_id(0)
    n = pl.program_id(2)
    blk = ATTN_BLOCK
    si = lax.broadcasted_iota(jnp.int32, (1, 2 * blk), 1)
    first = jnp.where((si >= blk) | (n > 0), 0.0, NEG_INF)
    outs = []
    for j in range(kvp):
        kk = jnp.concatenate([kp_ref[0, j], kc_ref[0, j]], axis=0)
        vv = jnp.concatenate([vp_ref[0, j], vc_ref[0, j]], axis=0)
        for g in range(group):
            hq = j * group + g
            qh = q_ref[0, :, hq * HEAD_DIM:(hq + 1) * HEAD_DIM] * ATTN_SCALE
            s = _dot_nt(qh, kk) + bias_ref[hq] + first
            sink = sink_ref[kvh * kvp * group + hq]
            m = jnp.maximum(jnp.max(s, axis=-1, keepdims=True), sink)
            p = jnp.exp(s - m)
            w = p / (jnp.sum(p, axis=-1, keepdims=True) + jnp.exp(sink - m))
            outs.append(_dot(w.astype(MXU_DTYPE), vv))
    o_ref[0] = jnp.concatenate(outs, axis=-1).astype(o_ref.dtype)


def _attention(q, k_sh, v_sh, sinks, kvp):
    bsz, seq, dq = q.shape
    group = dq // (KV_HEADS * HEAD_DIM)
    blk = ATTN_BLOCK
    gw = kvp * group * HEAD_DIM
    cur = pl.BlockSpec((1, kvp, blk, HEAD_DIM), lambda k, b, n: (b, k, n, 0))
    prev = pl.BlockSpec((1, kvp, blk, HEAD_DIM), lambda k, b, n: (b, k, jnp.maximum(n - 1, 0), 0))
    nheads = KV_HEADS * group
    slopes = jnp.asarray([2.0 ** (-8.0 * h / nheads) for h in range(1, nheads + 1)], F32)
    dist = (jnp.arange(blk)[:, None] + blk - jnp.arange(2 * blk)[None, :])
    bias = jnp.where((dist >= 0) & (dist < blk),
                     -(slopes[:, None, None] * dist.astype(F32)[None]), NEG_INF)
    return pl.pallas_call(
        functools.partial(_attn_kernel, group=group, kvp=kvp),
        grid=(KV_HEADS // kvp, bsz, seq // blk),
        in_specs=[pl.BlockSpec((1, blk, gw), lambda k, b, n: (b, n, k)),
                  prev, cur, prev, cur,
                  pl.BlockSpec((kvp * group, blk, 2 * blk), lambda k, b, n: (k, 0, 0)),
                  pl.BlockSpec(memory_space=pltpu.SMEM)],
        out_specs=pl.BlockSpec((1, blk, gw), lambda k, b, n: (b, n, k)),
        out_shape=jax.ShapeDtypeStruct((bsz, seq, dq), MXU_DTYPE),
        compiler_params=_params(("parallel", "parallel", "arbitrary")),
        name="swa_attention",
    )(q, k_sh, k_sh, v_sh, v_sh, bias, sinks)


def _top16(s):
    rid = lax.broadcasted_iota(jnp.int32, (PEER_TOPK, s.shape[1]), 0)
    tops = jnp.zeros((PEER_TOPK, s.shape[1]), F32)
    work = s
    for kth in range(PEER_TOPK):
        m = jnp.max(work, axis=0, keepdims=True)
        tops = jnp.where(rid == kth, m, tops)
        work = jnp.where(work == m, -jnp.inf, work)
    return tops


def _peer_sel_kernel(x_ref, wq_ref, sub_ref, s1_ref, s2_ref, st_ref, *, tb, hps):
    qd = 2 * PEER_HALF
    qall = _dot(x_ref[...].astype(MXU_DTYPE), wq_ref[...]).astype(MXU_DTYPE)
    for j in range(hps):
        q = qall[:, j * qd:(j + 1) * qd]
        s1 = _dot_nt(sub_ref[j, 0], q[:, :PEER_HALF])
        s2 = _dot_nt(sub_ref[j, 1], q[:, PEER_HALF:])
        for lb in range(tb // LANES):
            a1 = s1[:, lb * LANES:(lb + 1) * LANES]
            a2 = s2[:, lb * LANES:(lb + 1) * LANES]
            s1_ref[lb, j] = a1
            s2_ref[lb, j] = a2
            t1 = _top16(a1)
            t2 = _top16(a2)
            cands = [t1[0:1] + t2]
            for a in range(1, SUBLANES):
                cands.append(t1[a:a + 1] + t2[0:SUBLANES])
            cands.append(t1[SUBLANES:] + t2[0:1])
            cand = jnp.concatenate(cands, axis=0)
            best = _top16(cand)
            top = best[0:1]
            tau = best[PEER_TOPK - 1:PEER_TOPK]
            z = jnp.sum(jnp.where(cand >= tau, jnp.exp(cand - top), 0.0), axis=0, keepdims=True)
            zero = jnp.zeros_like(z)
            st_ref[lb, j] = jnp.concatenate(
                [tau, t1[0:1], t2[0:1], 1.0 / z, zero, zero, zero, zero], axis=0)


def _peer_select(x, w_q, subkeys, tb, hps):
    t, d = x.shape
    tb = min(tb, t)
    nlb = tb // LANES
    qd = 2 * PEER_HALF
    blk = lambda rows: pl.BlockSpec((nlb, hps, rows, LANES), lambda i, h: (i, h, 0, 0))
    shp = lambda rows: jax.ShapeDtypeStruct((t // LANES, PEER_HEADS, rows, LANES), F32)
    return pl.pallas_call(
        functools.partial(_peer_sel_kernel, tb=tb, hps=hps),
        grid=(t // tb, PEER_HEADS // hps),
        in_specs=[pl.BlockSpec((tb, d), lambda i, h: (i, 0)),
                  pl.BlockSpec((d, hps * qd), lambda i, h: (0, h)),
                  pl.BlockSpec((hps, 2, PEER_NKEYS, PEER_HALF), lambda i, h: (h, 0, 0, 0))],
        out_specs=[blk(PEER_NKEYS), blk(PEER_NKEYS), blk(SUBLANES)],
        out_shape=[shp(PEER_NKEYS), shp(PEER_NKEYS), shp(SUBLANES)],
        compiler_params=_params(("parallel", "arbitrary")),
        name="peer_select",
    )(x, w_q, subkeys)


def _gelu(x):
    return 0.5 * x * (1.0 + lax.erf(x * (1.0 / math.sqrt(2.0))))


def _peer_dense_kernel(x_ref, u_ref, v_ref, s1_ref, s2_ref, st_ref, o_ref,
                       e2_ref, w1_ref, g_ref, at_ref, *, tb, ec):
    e = pl.program_id(1)
    nlb = tb // LANES
    nsub = ec // PEER_NKEYS

    @pl.when(e == 0)
    def _():
        o_ref[...] = jnp.zeros(o_ref.shape, F32)
        for lb in range(nlb):
            for h in range(PEER_HEADS):
                e2_ref[lb, h] = jnp.exp(s2_ref[lb, h] - st_ref[lb, h, 2:3, :])
                w1_ref[lb, h] = jnp.exp(s1_ref[lb, h] - st_ref[lb, h, 1:2, :]) * st_ref[lb, h, 3:4, :]

    for ii in range(nsub):
        i = e * nsub + ii
        for lb in range(nlb):
            lanes = slice(lb * LANES, (lb + 1) * LANES)
            for kt in range(PEER_NKEYS // GATE_ROWS):
                keys = slice(kt * GATE_ROWS, (kt + 1) * GATE_ROWS)
                gate = jnp.zeros((GATE_ROWS, LANES), F32)
                for h in range(PEER_HEADS):
                    tau = st_ref[lb, h, 0:1, :]
                    cs = s1_ref[lb, h, pl.ds(i, 1), :] + s2_ref[lb, h, keys, :]
                    val = e2_ref[lb, h, keys, :] * w1_ref[lb, h, pl.ds(i, 1), :]
                    gate = gate + jnp.where(cs >= tau, val, 0.0)
                g_ref[ii * PEER_NKEYS + kt * GATE_ROWS:ii * PEER_NKEYS + (kt + 1) * GATE_ROWS,
                      lanes] = gate

    ht = _dot_nt(u_ref[...], x_ref[...])
    at_ref[...] = (_gelu(ht) * g_ref[...]).astype(at_ref.dtype)
    o_ref[...] += _dot_tn(at_ref[...], v_ref[...])


def _peer_dense(x, u, v, layer, s1, s2, st, tb, ec):
    t, d = x.shape
    tb = min(tb, t)
    nlb = tb // LANES
    ne = u.shape[1] // ec
    once = pl.Buffered(1)
    sblk = lambda rows: pl.BlockSpec((nlb, PEER_HEADS, rows, LANES), lambda i, e: (i, 0, 0, 0),
                                     pipeline_mode=once)
    return pl.pallas_call(
        functools.partial(_peer_dense_kernel, tb=tb, ec=ec),
        grid=(t // tb, ne),
        in_specs=[pl.BlockSpec((tb, d), lambda i, e: (i, 0), pipeline_mode=once),
                  pl.BlockSpec((None, ec, d), lambda i, e: (layer, e, 0)),
                  pl.BlockSpec((None, ec, d), lambda i, e: (layer, e, 0)),
                  sblk(PEER_NKEYS), sblk(PEER_NKEYS), sblk(SUBLANES)],
        out_specs=pl.BlockSpec((tb, d), lambda i, e: (i, 0), pipeline_mode=once),
        out_shape=jax.ShapeDtypeStruct((t, d), F32),
        scratch_shapes=[pltpu.VMEM((nlb, PEER_HEADS, PEER_NKEYS, LANES), F32),
                        pltpu.VMEM((nlb, PEER_HEADS, PEER_NKEYS, LANES), F32),
                        pltpu.VMEM((ec, tb), F32),
                        pltpu.VMEM((ec, tb), MXU_DTYPE)],
        compiler_params=_params(("parallel", "arbitrary")),
        name="peer_dense",
    )(x, u, v, s1, s2, st)


def _peer_layer(x, xb, w_q, subkeys, u, v, layer, gain, bias, with_copy):
    s1, s2, st = _peer_select(xb, w_q, subkeys, tb=512, hps=2)
    y = _peer_dense(xb, u, v, layer, s1, s2, st, tb=512, ec=1024)
    return _res_ln(x, y, gain, bias, tm=256, with_copy=with_copy)


def kernel(x, hgrn_w_in, hgrn_lb_logits, hgrn_norm_gain, hgrn_w_out, kv_w, attn_w_q, attn_sinks,
           attn_w_out, peer_w_q, peer_subkeys, peer_u, peer_v, ln_gain, ln_bias):
    bsz, seq, d = x.shape
    t = bsz * seq
    cdt = MXU_DTYPE
    dk = HGRN_DK
    nh = d // dk

    o = _hgrn(x.astype(cdt), hgrn_w_in[0].astype(cdt), hgrn_lb_logits, hgrn_norm_gain[0],
              layer=0, ts=512, hp=4)
    xt = x.reshape(t, d)
    x1, x1b = _matmul_ln(o.reshape(t, d), hgrn_w_out[0].astype(cdt), xt, ln_gain[0, 0], ln_bias[0, 0],
                         tm=512, tn=1024)
    u_all = peer_u.astype(cdt)
    v_all = peer_v.astype(cdt)
    x2, x2b = _peer_layer(x1, x1b, peer_w_q[0].astype(cdt), peer_subkeys[0].astype(cdt),
                          u_all, v_all, 0, ln_gain[0, 1], ln_bias[0, 1], with_copy=True)

    k_sh, v_sh = _kv_proj(x2b.reshape(bsz, seq, d), kv_w.astype(cdt), tm=512)

    q = _matmul(x2b, attn_w_q[0].astype(cdt), cdt, tm=512, tn=1024)
    att = _attention(q.reshape(bsz, seq, -1), k_sh, v_sh, attn_sinks[0].astype(F32), kvp=4)
    x3, x3b = _matmul_ln(att.reshape(t, -1), attn_w_out[0].astype(cdt), x2, ln_gain[1, 0], ln_bias[1, 0],
                         tm=512, tn=1024)
    (x4,) = _peer_layer(x3, x3b, peer_w_q[1].astype(cdt), peer_subkeys[1].astype(cdt),
                        u_all, v_all, 1, ln_gain[1, 1], ln_bias[1, 1], with_copy=False)
    return x4.reshape(bsz, seq, d)
```

```python
import functools
import math

import jax
import jax.numpy as jnp
from jax import lax
from jax.experimental import pallas as pl
from jax.experimental.pallas import tpu as pltpu

F32 = jnp.float32
MXU_DTYPE = jnp.bfloat16

DEPTH = 2
ALPHA = (2.0 * DEPTH) ** 0.25
LN_EPS = 1e-5
LN_UNROLL = 4
RMS_EPS = 1e-6

HGRN_DK = 128
HGRN_CHUNK = 32

HEAD_DIM = 64
KV_HEADS = 8
ATTN_BLOCK = 128
ATTN_SCALE = HEAD_DIM ** -0.5
NEG_INF = -1e30

PEER_HEADS = 8
PEER_NKEYS = 128
PEER_TOPK = 16
PEER_HALF = 128
GATE_ROWS = 32

LANES = 128
SUBLANES = 8
VMEM_LIMIT = 60 * 1024 * 1024


def _params(sem, flags=None):
    return pltpu.CompilerParams(dimension_semantics=sem, vmem_limit_bytes=VMEM_LIMIT, flags=flags)


def _sigmoid(x):
    return 1.0 / (1.0 + jnp.exp(-x))


def _dot(a, b):
    return jnp.dot(a, b, preferred_element_type=F32)


def _dot_nt(a, b):
    return lax.dot_general(a, b, (((1,), (1,)), ((), ())), preferred_element_type=F32)


def _dot_tn(a, b):
    return lax.dot_general(a, b, (((0,), (0,)), ((), ())), preferred_element_type=F32)


def _mm_kernel(a_ref, w_ref, o_ref):
    a = a_ref[...].astype(MXU_DTYPE)
    o_ref[...] = _dot(a, w_ref[...]).astype(o_ref.dtype)


def _matmul(a, w, out_dtype, tm, tn):
    m, k = a.shape
    n = w.shape[1]
    tm, tn = min(tm, m), min(tn, n)
    return pl.pallas_call(
        _mm_kernel,
        grid=(m // tm, n // tn),
        in_specs=[pl.BlockSpec((tm, k), lambda i, j: (i, 0)),
                  pl.BlockSpec((k, tn), lambda i, j: (0, j))],
        out_specs=pl.BlockSpec((tm, tn), lambda i, j: (i, j)),
        out_shape=jax.ShapeDtypeStruct((m, n), out_dtype),
        compiler_params=_params(("parallel", "arbitrary")),
        name="matmul",
    )(a, w)


def _kv_kernel(a_ref, w_ref, k_ref, v_ref):
    y = _dot(a_ref[0].astype(MXU_DTYPE), w_ref[...])
    nkv = KV_HEADS * HEAD_DIM
    for h in range(KV_HEADS):
        cols = slice(h * HEAD_DIM, (h + 1) * HEAD_DIM)
        k_ref[0, h] = y[:, cols].astype(k_ref.dtype)
        v_ref[0, h] = y[:, nkv:][:, cols].astype(v_ref.dtype)


def _kv_proj(x, w, tm):
    bsz, seq, d = x.shape
    tm = min(tm, seq)
    out = pl.BlockSpec((1, KV_HEADS, tm, HEAD_DIM), lambda b, i: (b, 0, i, 0))
    shape = jax.ShapeDtypeStruct((bsz, KV_HEADS, seq, HEAD_DIM), MXU_DTYPE)
    return pl.pallas_call(
        _kv_kernel,
        grid=(bsz, seq // tm),
        in_specs=[pl.BlockSpec((1, tm, d), lambda b, i: (b, i, 0)),
                  pl.BlockSpec(w.shape, lambda b, i: (0, 0))],
        out_specs=[out, out],
        out_shape=[shape, shape],
        compiler_params=_params(("parallel", "arbitrary")),
        name="kv_proj",
    )(x, w)


def _row_mean(z):
    n = z.shape[-1]
    return jnp.sum(_tree_sum(_lane_groups(z)), axis=-1, keepdims=True) * (1.0 / n)


def _lane_groups(z):
    return [z[:, k * LANES:(k + 1) * LANES] for k in range(z.shape[-1] // LANES)]


def _tree_sum(parts):
    while len(parts) > 1:
        parts = [a + b for a, b in zip(parts[0::2], parts[1::2])] + parts[len(parts) & ~1:]
    return parts[0]


def _ln_rows(z, g, b):
    zc = z - _row_mean(z)
    var = _row_mean(zc * zc)
    return zc * lax.rsqrt(var + LN_EPS) * g + b


def _mm_ln_kernel(a_ref, w_ref, res_ref, g_ref, b_ref, o_ref, ob_ref, mu_ref, rs_ref,
                  *, tn, nj, rows):
    j = pl.program_id(1)
    y = _dot(a_ref[...].astype(MXU_DTYPE), w_ref[...]) + ALPHA * res_ref[...]
    for jj in range(nj):
        @pl.when(j == jj)
        def _(jj=jj):
            o_ref[:, jj * tn:(jj + 1) * tn] = y

    @pl.when(j == nj - 1)
    def _():
        ngroups = o_ref.shape[0] // rows
        n = o_ref.shape[1]
        group = lambda r: pl.ds(pl.multiple_of(r * rows, rows), rows)

        def mean_pass(r, carry):
            sl = group(r)
            mu_ref[sl, :] = jnp.broadcast_to(_row_mean(o_ref[sl, :]), (rows, LANES))
            return carry

        def var_pass(r, carry):
            sl = group(r)
            mu = mu_ref[sl, :]
            sq = [(zk - mu) * (zk - mu) for zk in _lane_groups(o_ref[sl, :])]
            var = jnp.sum(_tree_sum(sq), axis=-1, keepdims=True) * (1.0 / n)
            rs_ref[sl, :] = jnp.broadcast_to(lax.rsqrt(var + LN_EPS), (rows, LANES))
            return carry

        def norm_pass(r, carry):
            sl = group(r)
            mu = mu_ref[sl, :]
            rs = rs_ref[sl, :]
            for k, zk in enumerate(_lane_groups(o_ref[sl, :])):
                cols = slice(k * LANES, (k + 1) * LANES)
                out = (zk - mu) * rs * g_ref[:, cols] + b_ref[:, cols]
                o_ref[sl, cols] = out
                ob_ref[sl, cols] = out.astype(ob_ref.dtype)
            return carry

        lax.fori_loop(0, ngroups, mean_pass, 0, unroll=2 * LN_UNROLL)
        lax.fori_loop(0, ngroups, var_pass, 0, unroll=2 * LN_UNROLL)
        lax.fori_loop(0, ngroups, norm_pass, 0, unroll=LN_UNROLL)


def _matmul_ln(a, w, res, gain, bias, tm, tn):
    m, k = a.shape
    n = w.shape[1]
    tm, tn = min(tm, m), min(tn, n)
    nj = n // tn
    full = pl.BlockSpec((tm, n), lambda i, j: (i, 0))
    return pl.pallas_call(
        functools.partial(_mm_ln_kernel, tn=tn, nj=nj, rows=SUBLANES),
        grid=(m // tm, nj),
        in_specs=[pl.BlockSpec((tm, k), lambda i, j: (i, 0)),
                  pl.BlockSpec((k, tn), lambda i, j: (0, j)),
                  pl.BlockSpec((tm, tn), lambda i, j: (i, j)),
                  pl.BlockSpec((1, n), lambda i, j: (0, 0)),
                  pl.BlockSpec((1, n), lambda i, j: (0, 0))],
        out_specs=[full, full],
        out_shape=[jax.ShapeDtypeStruct((m, n), F32), jax.ShapeDtypeStruct((m, n), MXU_DTYPE)],
        scratch_shapes=[pltpu.VMEM((tm, LANES), F32), pltpu.VMEM((tm, LANES), F32)],
        compiler_params=_params(("parallel", "arbitrary")),
        name="matmul_ln",
    )(a, w, res, gain.reshape(1, n), bias.reshape(1, n))


def _res_ln_kernel(x_ref, y_ref, g_ref, b_ref, *o_refs):
    z = _ln_rows(ALPHA * x_ref[...] + y_ref[...], g_ref[...], b_ref[...])
    for o_ref in o_refs:
        o_ref[...] = z.astype(o_ref.dtype)


def _res_ln(x, y, gain, bias, tm, with_copy):
    m, n = x.shape
    tm = min(tm, m)
    row = pl.BlockSpec((tm, n), lambda i: (i, 0))
    vec = pl.BlockSpec((1, n), lambda i: (0, 0))
    dtypes = (F32, MXU_DTYPE) if with_copy else (F32,)
    return pl.pallas_call(
        _res_ln_kernel,
        grid=(m // tm,),
        in_specs=[row, row, vec, vec],
        out_specs=[row] * len(dtypes),
        out_shape=[jax.ShapeDtypeStruct((m, n), dt) for dt in dtypes],
        compiler_params=_params(("parallel",)),
        name="res_ln",
    )(x, y, gain.reshape(1, n), bias.reshape(1, n))


def _hgrn_kernel(x_ref, wq_ref, wf_ref, wi_ref, wg_ref, lbl_ref, gain_ref, o_ref,
                 state_ref, qg_ref, kg_ref, k_ref, b_ref, v_ref, g_ref, *, layer, ts, hp):
    sblk = pl.program_id(1)
    hb = pl.program_id(2)
    dk = HGRN_DK
    c = HGRN_CHUNK

    @pl.when(sblk == 0)
    def _():
        for j in range(hp):
            state_ref[hb * hp + j] = jnp.zeros((dk, dk), F32)

    x = x_ref[0]
    pq, pf, pi, pg = (_dot(x, w[...]) for w in (wq_ref, wf_ref, wi_ref, wg_ref))
    row = lax.broadcasted_iota(jnp.int32, (ts, dk), 0) % c
    for j in range(hp):
        cols = slice(j * dk, (j + 1) * dk)
        lg = lbl_ref[j]
        ex = jnp.exp(lg - jnp.max(lg, axis=0, keepdims=True))
        lb = jnp.sum(ex[:layer + 1], axis=0, keepdims=True) / jnp.sum(ex, axis=0, keepdims=True)

        q_raw = pq[:, cols]
        f_raw = pf[:, cols]
        f = lb + (1.0 - lb) * _sigmoid(f_raw)
        k = 1.0 - f
        b = jnp.log(f)
        sh = 1
        while sh < c:
            b = b + jnp.where(row >= sh, pltpu.roll(b, sh, axis=0), 0.0)
            sh *= 2
        qg_ref[j] = q_raw * _sigmoid(q_raw) * jnp.exp(b)
        kg_ref[j] = k * jnp.exp(-b)
        k_ref[j] = k
        b_ref[j] = b
        v_ref[j] = pi[:, cols]
        g_ref[j] = pg[:, cols]

    causal = (lax.broadcasted_iota(jnp.int32, (c, c), 0)
              >= lax.broadcasted_iota(jnp.int32, (c, c), 1))
    gain = gain_ref[...]

    def chunk(j, ci, st):
        sl = slice(ci * c, (ci + 1) * c)
        qg = qg_ref[j, sl, :].astype(MXU_DTYPE)
        vv = v_ref[j, sl, :].astype(MXU_DTYPE)
        bc = b_ref[j, sl, :]
        bl = bc[c - 1:c, :]
        a = jnp.where(causal, _dot_nt(qg, kg_ref[j, sl, :].astype(MXU_DTYPE)), 0.0)
        o = _dot(a.astype(MXU_DTYPE), vv) + _dot_nt(qg, st.astype(MXU_DTYPE))
        kd = (k_ref[j, sl, :] * jnp.exp(bl - bc)).astype(MXU_DTYPE)
        st = st * jnp.exp(bl) + _dot_tn(vv, kd)
        o = o * lax.rsqrt(jnp.mean(o * o, axis=-1, keepdims=True) + RMS_EPS) * gain
        gr = g_ref[j, sl, :]
        o_ref[0, sl, j * dk:(j + 1) * dk] = (o * (gr * _sigmoid(gr))).astype(o_ref.dtype)
        return st

    sts = [state_ref[hb * hp + j] for j in range(hp)]
    for ci in range(ts // c):
        sts = [chunk(j, ci, sts[j]) for j in range(hp)]
    for j in range(hp):
        state_ref[hb * hp + j] = sts[j]


def _hgrn(xb, w_in, lb_logits, gain, layer, ts, hp):
    bsz, seq, d = xb.shape
    dk = HGRN_DK
    nh = d // dk
    nhb = nh // hp
    ts = min(ts, seq)
    slots = lb_logits.shape[0]
    lbl = lb_logits.reshape(slots, nh, dk).transpose(1, 0, 2)
    buf = pltpu.VMEM((hp, ts, dk), F32)

    def wcols(part):
        return pl.BlockSpec((d, hp * dk), lambda b, s, h: (0, part * nhb + h))

    return pl.pallas_call(
        functools.partial(_hgrn_kernel, layer=layer, ts=ts, hp=hp),
        grid=(bsz, seq // ts, nhb),
        in_specs=[pl.BlockSpec((1, ts, d), lambda b, s, h: (b, s, 0)),
                  wcols(0), wcols(1), wcols(2), wcols(3),
                  pl.BlockSpec((hp, slots, dk), lambda b, s, h: (h, 0, 0)),
                  pl.BlockSpec((1, dk), lambda b, s, h: (0, 0))],
        out_specs=pl.BlockSpec((1, ts, hp * dk), lambda b, s, h: (b, s, h)),
        out_shape=jax.ShapeDtypeStruct((bsz, seq, d), MXU_DTYPE),
        scratch_shapes=[pltpu.VMEM((nh, dk, dk), F32), buf, buf, buf, buf, buf, buf],
        compiler_params=_params(("parallel", "arbitrary", "arbitrary")),
        name="hgrn2",
    )(xb, w_in, w_in, w_in, w_in, lbl, gain.reshape(1, dk))


def _attn_kernel(q_ref, kp_ref, kc_ref, vp_ref, vc_ref, bias_ref, sink_ref, o_ref, *, group, kvp):
    kvh = pl.program_id(0)
    n = pl.program_id(2)
    blk = ATTN_BLOCK
    si = lax.broadcasted_iota(jnp.int32, (1, 2 * blk), 1)
    first = jnp.where((si >= blk) | (n > 0), 0.0, NEG_INF)
    outs = []
    for j in range(kvp):
        kk = jnp.concatenate([kp_ref[0, j], kc_ref[0, j]], axis=0)
        vv = jnp.concatenate([vp_ref[0, j], vc_ref[0, j]], axis=0)
        for g in range(group):
            hq = j * group + g
            qh = q_ref[0, :, hq * HEAD_DIM:(hq + 1) * HEAD_DIM] * ATTN_SCALE
            s = _dot_nt(qh, kk) + bias_ref[hq] + first
            sink = sink_ref[kvh * kvp * group + hq]
            m = jnp.maximum(jnp.max(s, axis=-1, keepdims=True), sink)
            p = jnp.exp(s - m)
            w = p / (jnp.sum(p, axis=-1, keepdims=True) + jnp.exp(sink - m))
            outs.append(_dot(w.astype(MXU_DTYPE), vv))
    o_ref[0] = jnp.concatenate(outs, axis=-1).astype(o_ref.dtype)


def _attention(q, k_sh, v_sh, sinks, kvp):
    bsz, seq, dq = q.shape
    group = dq // (KV_HEADS * HEAD_DIM)
    blk = ATTN_BLOCK
    gw = kvp * group * HEAD_DIM
    cur = pl.BlockSpec((1, kvp, blk, HEAD_DIM), lambda k, b, n: (b, k, n, 0))
    prev = pl.BlockSpec((1, kvp, blk, HEAD_DIM), lambda k, b, n: (b, k, jnp.maximum(n - 1, 0), 0))
    nheads = KV_HEADS * group
    slopes = jnp.asarray([2.0 ** (-8.0 * h / nheads) for h in range(1, nheads + 1)], F32)
    dist = (jnp.arange(blk)[:, None] + blk - jnp.arange(2 * blk)[None, :])
    bias = jnp.where((dist >= 0) & (dist < blk),
                     -(slopes[:, None, None] * dist.astype(F32)[None]), NEG_INF)
    return pl.pallas_call(
        functools.partial(_attn_kernel, group=group, kvp=kvp),
        grid=(KV_HEADS // kvp, bsz, seq // blk),
        in_specs=[pl.BlockSpec((1, blk, gw), lambda k, b, n: (b, n, k)),
                  prev, cur, prev, cur,
                  pl.BlockSpec((kvp * group, blk, 2 * blk), lambda k, b, n: (k, 0, 0)),
                  pl.BlockSpec(memory_space=pltpu.SMEM)],
        out_specs=pl.BlockSpec((1, blk, gw), lambda k, b, n: (b, n, k)),
        out_shape=jax.ShapeDtypeStruct((bsz, seq, dq), MXU_DTYPE),
        compiler_params=_params(("parallel", "parallel", "arbitrary")),
        name="swa_attention",
    )(q, k_sh, k_sh, v_sh, v_sh, bias, sinks)


def _sort_network(n):
    pairs = []
    p = 1
    while p < n:
        k = p
        while k >= 1:
            for j in range(k % p, n - k, 2 * k):
                for i in range(min(k, n - j - k)):
                    if (i + j) // (2 * p) == (i + j + k) // (2 * p):
                        pairs.append((i + j, i + j + k))
            k //= 2
        p *= 2
    return pairs


def _compare_exchange(v, i, j):
    if v[j] is None:
        return
    if v[i] is None:
        v[i], v[j] = v[j], None
        return
    v[i], v[j] = jnp.maximum(v[i], v[j]), jnp.minimum(v[i], v[j])


def _top16(s):
    k = PEER_TOPK
    v = [s[r * SUBLANES:(r + 1) * SUBLANES] for r in range(s.shape[0] // SUBLANES)]
    v += [None] * (k - len(v))
    for i, j in _sort_network(k):
        _compare_exchange(v, i, j)
    shift = SUBLANES // 2
    while shift >= 1:
        other = [None if a is None else pltpu.roll(a, shift, axis=0) for a in v]
        for i in range(k):
            a, b = v[i], other[k - 1 - i]
            v[i] = b if a is None else a if b is None else jnp.maximum(a, b)
        stride = k // 2
        while stride >= 1:
            for i in range(k):
                if i & stride == 0:
                    _compare_exchange(v, i, i + stride)
            stride //= 2
        shift //= 2
    return jnp.concatenate([a[0:1] for a in v], axis=0)


def _peer_sel_kernel(x_ref, wq_ref, sub_ref, s1_ref, s2_ref, st_ref, *, tb, hps):
    qd = 2 * PEER_HALF
    qall = _dot(x_ref[...].astype(MXU_DTYPE), wq_ref[...]).astype(MXU_DTYPE)
    for j in range(hps):
        q = qall[:, j * qd:(j + 1) * qd]
        s1 = _dot_nt(sub_ref[j, 0], q[:, :PEER_HALF])
        s2 = _dot_nt(sub_ref[j, 1], q[:, PEER_HALF:])
        for lb in range(tb // LANES):
            a1 = s1[:, lb * LANES:(lb + 1) * LANES]
            a2 = s2[:, lb * LANES:(lb + 1) * LANES]
            s1_ref[lb, j] = a1
            s2_ref[lb, j] = a2
            t1 = _top16(a1)
            t2 = _top16(a2)
            cands = [t1[0:1] + t2]
            for a in range(1, SUBLANES):
                cands.append(t1[a:a + 1] + t2[0:SUBLANES])
            cands.append(t1[SUBLANES:] + t2[0:1])
            cand = jnp.concatenate(cands, axis=0)
            best = _top16(cand)
            top = best[0:1]
            tau = best[PEER_TOPK - 1:PEER_TOPK]
            z = jnp.sum(jnp.where(cand >= tau, jnp.exp(cand - top), 0.0), axis=0, keepdims=True)
            zero = jnp.zeros_like(z)
            st_ref[lb, j] = jnp.concatenate(
                [tau, t1[0:1], t2[0:1], 1.0 / z, zero, zero, zero, zero], axis=0)


def _peer_select(x, w_q, subkeys, tb, hps):
    t, d = x.shape
    tb = min(tb, t)
    nlb = tb // LANES
    qd = 2 * PEER_HALF
    blk = lambda rows: pl.BlockSpec((nlb, hps, rows, LANES), lambda i, h: (i, h, 0, 0))
    shp = lambda rows: jax.ShapeDtypeStruct((t // LANES, PEER_HEADS, rows, LANES), F32)
    return pl.pallas_call(
        functools.partial(_peer_sel_kernel, tb=tb, hps=hps),
        grid=(t // tb, PEER_HEADS // hps),
        in_specs=[pl.BlockSpec((tb, d), lambda i, h: (i, 0)),
                  pl.BlockSpec((d, hps * qd), lambda i, h: (0, h)),
                  pl.BlockSpec((hps, 2, PEER_NKEYS, PEER_HALF), lambda i, h: (h, 0, 0, 0))],
        out_specs=[blk(PEER_NKEYS), blk(PEER_NKEYS), blk(SUBLANES)],
        out_shape=[shp(PEER_NKEYS), shp(PEER_NKEYS), shp(SUBLANES)],
        compiler_params=_params(("parallel", "arbitrary")),
        name="peer_select",
    )(x, w_q, subkeys)


def _gelu(x):
    return 0.5 * x * (1.0 + lax.erf(x * (1.0 / math.sqrt(2.0))))


def _peer_dense_kernel(x_ref, u_ref, v_ref, s1_ref, s2_ref, st_ref, o_ref,
                       e2_ref, w1_ref, g_ref, at_ref, *, tb, ec):
    e = pl.program_id(1)
    nlb = tb // LANES
    nsub = ec // PEER_NKEYS

    @pl.when(e == 0)
    def _():
        o_ref[...] = jnp.zeros(o_ref.shape, F32)
        for lb in range(nlb):
            for h in range(PEER_HEADS):
                e2_ref[lb, h] = jnp.exp(s2_ref[lb, h] - st_ref[lb, h, 2:3, :])
                w1_ref[lb, h] = jnp.exp(s1_ref[lb, h] - st_ref[lb, h, 1:2, :]) * st_ref[lb, h, 3:4, :]

    for ii in range(nsub):
        i = e * nsub + ii
        for lb in range(nlb):
            lanes = slice(lb * LANES, (lb + 1) * LANES)
            for kt in range(PEER_NKEYS // GATE_ROWS):
                keys = slice(kt * GATE_ROWS, (kt + 1) * GATE_ROWS)
                gate = jnp.zeros((GATE_ROWS, LANES), F32)
                for h in range(PEER_HEADS):
                    tau = st_ref[lb, h, 0:1, :]
                    cs = s1_ref[lb, h, pl.ds(i, 1), :] + s2_ref[lb, h, keys, :]
                    val = e2_ref[lb, h, keys, :] * w1_ref[lb, h, pl.ds(i, 1), :]
                    gate = gate + jnp.where(cs >= tau, val, 0.0)
                g_ref[ii * PEER_NKEYS + kt * GATE_ROWS:ii * PEER_NKEYS + (kt + 1) * GATE_ROWS,
                      lanes] = gate

    ht = _dot_nt(u_ref[...], x_ref[...])
    at_ref[...] = (_gelu(ht) * g_ref[...]).astype(at_ref.dtype)
    o_ref[...] += _dot_tn(at_ref[...], v_ref[...])


def _peer_dense(x, u, v, layer, s1, s2, st, tb, ec):
    t, d = x.shape
    tb = min(tb, t)
    nlb = tb // LANES
    ne = u.shape[1] // ec
    once = pl.Buffered(1)
    sblk = lambda rows: pl.BlockSpec((nlb, PEER_HEADS, rows, LANES), lambda i, e: (i, 0, 0, 0),
                                     pipeline_mode=once)
    return pl.pallas_call(
        functools.partial(_peer_dense_kernel, tb=tb, ec=ec),
        grid=(t // tb, ne),
        in_specs=[pl.BlockSpec((tb, d), lambda i, e: (i, 0), pipeline_mode=once),
                  pl.BlockSpec((None, ec, d), lambda i, e: (layer, e, 0)),
                  pl.BlockSpec((None, ec, d), lambda i, e: (layer, e, 0)),
                  sblk(PEER_NKEYS), sblk(PEER_NKEYS), sblk(SUBLANES)],
        out_specs=pl.BlockSpec((tb, d), lambda i, e: (i, 0), pipeline_mode=once),
        out_shape=jax.ShapeDtypeStruct((t, d), F32),
        scratch_shapes=[pltpu.VMEM((nlb, PEER_HEADS, PEER_NKEYS, LANES), F32),
                        pltpu.VMEM((nlb, PEER_HEADS, PEER_NKEYS, LANES), F32),
                        pltpu.VMEM((ec, tb), F32),
                        pltpu.VMEM((ec, tb), MXU_DTYPE)],
        compiler_params=_params(("parallel", "arbitrary")),
        name="peer_dense",
    )(x, u, v, s1, s2, st)


def _peer_layer(x, xb, w_q, subkeys, u, v, layer, gain, bias, with_copy):
    s1, s2, st = _peer_select(xb, w_q, subkeys, tb=512, hps=2)
    y = _peer_dense(xb, u, v, layer, s1, s2, st, tb=512, ec=1024)
    return _res_ln(x, y, gain, bias, tm=256, with_copy=with_copy)


def kernel(x, hgrn_w_in, hgrn_lb_logits, hgrn_norm_gain, hgrn_w_out, kv_w, attn_w_q, attn_sinks,
           attn_w_out, peer_w_q, peer_subkeys, peer_u, peer_v, ln_gain, ln_bias):
    bsz, seq, d = x.shape
    t = bsz * seq
    cdt = MXU_DTYPE
    dk = HGRN_DK
    nh = d // dk

    o = _hgrn(x.astype(cdt), hgrn_w_in[0].astype(cdt), hgrn_lb_logits, hgrn_norm_gain[0],
              layer=0, ts=512, hp=4)
    xt = x.reshape(t, d)
    x1, x1b = _matmul_ln(o.reshape(t, d), hgrn_w_out[0].astype(cdt), xt, ln_gain[0, 0], ln_bias[0, 0],
                         tm=512, tn=1024)
    u_all = peer_u.astype(cdt)
    v_all = peer_v.astype(cdt)
    x2, x2b = _peer_layer(x1, x1b, peer_w_q[0].astype(cdt), peer_subkeys[0].astype(cdt),
                          u_all, v_all, 0, ln_gain[0, 1], ln_bias[0, 1], with_copy=True)

    k_sh, v_sh = _kv_proj(x2b.reshape(bsz, seq, d), kv_w.astype(cdt), tm=512)

    q = _matmul(x2b, attn_w_q[0].astype(cdt), cdt, tm=512, tn=1024)
    att = _attention(q.reshape(bsz, seq, -1), k_sh, v_sh, attn_sinks[0].astype(F32), kvp=4)
    x3, x3b = _matmul_ln(att.reshape(t, -1), attn_w_out[0].astype(cdt), x2, ln_gain[1, 0], ln_bias[1, 0],
                         tm=512, tn=1024)
    (x4,) = _peer_layer(x3, x3b, peer_w_q[1].astype(cdt), peer_subkeys[1].astype(cdt),
                        u_all, v_all, 1, ln_gain[1, 1], ln_bias[1, 1], with_copy=False)
    return x4.reshape(bsz, seq, d)
```

```python
import functools
import math

import jax
import jax.numpy as jnp
from jax import lax
from jax.experimental import pallas as pl
from jax.experimental.pallas import tpu as pltpu

F32 = jnp.float32
MXU_DTYPE = jnp.bfloat16

DEPTH = 2
ALPHA = (2.0 * DEPTH) ** 0.25
LN_EPS = 1e-5
LN_UNROLL = 4
RMS_EPS = 1e-6

HGRN_DK = 128
HGRN_CHUNK = 32

HEAD_DIM = 64
KV_HEADS = 8
ATTN_BLOCK = 128
ATTN_SCALE = HEAD_DIM ** -0.5
NEG_INF = -1e30

PEER_HEADS = 8
PEER_NKEYS = 128
PEER_TOPK = 16
PEER_HALF = 128
GATE_ROWS = 32

LANES = 128
SUBLANES = 8
VMEM_LIMIT = 60 * 1024 * 1024


def _params(sem, flags=None):
    return pltpu.CompilerParams(dimension_semantics=sem, vmem_limit_bytes=VMEM_LIMIT, flags=flags)


def _sigmoid(x):
    return 1.0 / (1.0 + jnp.exp(-x))


def _dot(a, b):
    return jnp.dot(a, b, preferred_element_type=F32)


def _dot_nt(a, b):
    return lax.dot_general(a, b, (((1,), (1,)), ((), ())), preferred_element_type=F32)


def _dot_tn(a, b):
    return lax.dot_general(a, b, (((0,), (0,)), ((), ())), preferred_element_type=F32)


def _mm_kernel(a_ref, w_ref, o_ref):
    a = a_ref[...].astype(MXU_DTYPE)
    o_ref[...] = _dot(a, w_ref[...]).astype(o_ref.dtype)


def _matmul(a, w, out_dtype, tm, tn):
    m, k = a.shape
    n = w.shape[1]
    tm, tn = min(tm, m), min(tn, n)
    return pl.pallas_call(
        _mm_kernel,
        grid=(m // tm, n // tn),
        in_specs=[pl.BlockSpec((tm, k), lambda i, j: (i, 0)),
                  pl.BlockSpec((k, tn), lambda i, j: (0, j))],
        out_specs=pl.BlockSpec((tm, tn), lambda i, j: (i, j)),
        out_shape=jax.ShapeDtypeStruct((m, n), out_dtype),
        compiler_params=_params(("parallel", "arbitrary")),
        name="matmul",
    )(a, w)


def _kv_kernel(a_ref, w_ref, k_ref, v_ref):
    y = _dot(a_ref[0].astype(MXU_DTYPE), w_ref[...])
    nkv = KV_HEADS * HEAD_DIM
    for h in range(KV_HEADS):
        cols = slice(h * HEAD_DIM, (h + 1) * HEAD_DIM)
        k_ref[0, h] = y[:, cols].astype(k_ref.dtype)
        v_ref[0, h] = y[:, nkv:][:, cols].astype(v_ref.dtype)


def _kv_proj(x, w, tm):
    bsz, seq, d = x.shape
    tm = min(tm, seq)
    out = pl.BlockSpec((1, KV_HEADS, tm, HEAD_DIM), lambda b, i: (b, 0, i, 0))
    shape = jax.ShapeDtypeStruct((bsz, KV_HEADS, seq, HEAD_DIM), MXU_DTYPE)
    return pl.pallas_call(
        _kv_kernel,
        grid=(bsz, seq // tm),
        in_specs=[pl.BlockSpec((1, tm, d), lambda b, i: (b, i, 0)),
                  pl.BlockSpec(w.shape, lambda b, i: (0, 0))],
        out_specs=[out, out],
        out_shape=[shape, shape],
        compiler_params=_params(("parallel", "arbitrary")),
        name="kv_proj",
    )(x, w)


def _row_mean(z):
    n = z.shape[-1]
    return jnp.sum(_tree_sum(_lane_groups(z)), axis=-1, keepdims=True) * (1.0 / n)


def _lane_groups(z):
    return [z[:, k * LANES:(k + 1) * LANES] for k in range(z.shape[-1] // LANES)]


def _tree_sum(parts):
    while len(parts) > 1:
        parts = [a + b for a, b in zip(parts[0::2], parts[1::2])] + parts[len(parts) & ~1:]
    return parts[0]


def _ln_rows(z, g, b):
    zc = z - _row_mean(z)
    var = _row_mean(zc * zc)
    return zc * lax.rsqrt(var + LN_EPS) * g + b


def _mm_ln_kernel(a_ref, w_ref, res_ref, g_ref, b_ref, o_ref, ob_ref, mu_ref, rs_ref,
                  *, tn, nj, rows):
    j = pl.program_id(1)
    y = _dot(a_ref[...].astype(MXU_DTYPE), w_ref[...]) + ALPHA * res_ref[...]
    for jj in range(nj):
        @pl.when(j == jj)
        def _(jj=jj):
            o_ref[:, jj * tn:(jj + 1) * tn] = y

    @pl.when(j == nj - 1)
    def _():
        ngroups = o_ref.shape[0] // rows
        n = o_ref.shape[1]
        group = lambda r: pl.ds(pl.multiple_of(r * rows, rows), rows)

        def mean_pass(r, carry):
            sl = group(r)
            mu_ref[sl, :] = jnp.broadcast_to(_row_mean(o_ref[sl, :]), (rows, LANES))
            return carry

        def var_pass(r, carry):
            sl = group(r)
            mu = mu_ref[sl, :]
            sq = [(zk - mu) * (zk - mu) for zk in _lane_groups(o_ref[sl, :])]
            var = jnp.sum(_tree_sum(sq), axis=-1, keepdims=True) * (1.0 / n)
            rs_ref[sl, :] = jnp.broadcast_to(lax.rsqrt(var + LN_EPS), (rows, LANES))
            return carry

        def norm_pass(r, carry):
            sl = group(r)
            mu = mu_ref[sl, :]
            rs = rs_ref[sl, :]
            for k, zk in enumerate(_lane_groups(o_ref[sl, :])):
                cols = slice(k * LANES, (k + 1) * LANES)
                out = (zk - mu) * rs * g_ref[:, cols] + b_ref[:, cols]
                o_ref[sl, cols] = out
                ob_ref[sl, cols] = out.astype(ob_ref.dtype)
            return carry

        lax.fori_loop(0, ngroups, mean_pass, 0, unroll=2 * LN_UNROLL)
        lax.fori_loop(0, ngroups, var_pass, 0, unroll=2 * LN_UNROLL)
        lax.fori_loop(0, ngroups, norm_pass, 0, unroll=LN_UNROLL)


def _matmul_ln(a, w, res, gain, bias, tm, tn):
    m, k = a.shape
    n = w.shape[1]
    tm, tn = min(tm, m), min(tn, n)
    nj = n // tn
    full = pl.BlockSpec((tm, n), lambda i, j: (i, 0))
    return pl.pallas_call(
        functools.partial(_mm_ln_kernel, tn=tn, nj=nj, rows=SUBLANES),
        grid=(m // tm, nj),
        in_specs=[pl.BlockSpec((tm, k), lambda i, j: (i, 0)),
                  pl.BlockSpec((k, tn), lambda i, j: (0, j)),
                  pl.BlockSpec((tm, tn), lambda i, j: (i, j)),
                  pl.BlockSpec((1, n), lambda i, j: (0, 0)),
                  pl.BlockSpec((1, n), lambda i, j: (0, 0))],
        out_specs=[full, full],
        out_shape=[jax.ShapeDtypeStruct((m, n), F32), jax.ShapeDtypeStruct((m, n), MXU_DTYPE)],
        scratch_shapes=[pltpu.VMEM((tm, LANES), F32), pltpu.VMEM((tm, LANES), F32)],
        compiler_params=_params(("parallel", "arbitrary")),
        name="matmul_ln",
    )(a, w, res, gain.reshape(1, n), bias.reshape(1, n))


def _res_ln_kernel(x_ref, y_ref, g_ref, b_ref, *o_refs):
    z = _ln_rows(ALPHA * x_ref[...] + y_ref[...], g_ref[...], b_ref[...])
    for o_ref in o_refs:
        o_ref[...] = z.astype(o_ref.dtype)


def _res_ln(x, y, gain, bias, tm, with_copy):
    m, n = x.shape
    tm = min(tm, m)
    row = pl.BlockSpec((tm, n), lambda i: (i, 0))
    vec = pl.BlockSpec((1, n), lambda i: (0, 0))
    dtypes = (F32, MXU_DTYPE) if with_copy else (F32,)
    return pl.pallas_call(
        _res_ln_kernel,
        grid=(m // tm,),
        in_specs=[row, row, vec, vec],
        out_specs=[row] * len(dtypes),
        out_shape=[jax.ShapeDtypeStruct((m, n), dt) for dt in dtypes],
        compiler_params=_params(("parallel",)),
        name="res_ln",
    )(x, y, gain.reshape(1, n), bias.reshape(1, n))


def _hgrn_kernel(x_ref, wq_ref, wf_ref, wi_ref, wg_ref, lbl_ref, gain_ref, o_ref,
                 state_ref, qg_ref, kg_ref, k_ref, b_ref, v_ref, g_ref, oi_ref, *, layer, ts, hp):
    sblk = pl.program_id(1)
    hb = pl.program_id(2)
    dk = HGRN_DK
    c = HGRN_CHUNK

    @pl.when(sblk == 0)
    def _():
        for j in range(hp):
            state_ref[hb * hp + j] = jnp.zeros((dk, dk), F32)

    x = x_ref[0]
    pq, pf, pi, pg = (_dot(x, w[...]) for w in (wq_ref, wf_ref, wi_ref, wg_ref))
    row = lax.broadcasted_iota(jnp.int32, (ts, dk), 0) % c
    for j in range(hp):
        cols = slice(j * dk, (j + 1) * dk)
        lg = lbl_ref[j]
        ex = jnp.exp(lg - jnp.max(lg, axis=0, keepdims=True))
        lb = jnp.sum(ex[:layer + 1], axis=0, keepdims=True) / jnp.sum(ex, axis=0, keepdims=True)

        q_raw = pq[:, cols]
        f_raw = pf[:, cols]
        f = lb + (1.0 - lb) * _sigmoid(f_raw)
        k = 1.0 - f
        b = jnp.log(f)
        sh = 1
        while sh < c:
            b = b + jnp.where(row >= sh, pltpu.roll(b, sh, axis=0), 0.0)
            sh *= 2
        qg_ref[j] = q_raw * _sigmoid(q_raw) * jnp.exp(b)
        kg_ref[j] = k * jnp.exp(-b)
        k_ref[j] = k
        b_ref[j] = b
        v_ref[j] = pi[:, cols]
        g_ref[j] = pg[:, cols]

    ri = lax.broadcasted_iota(jnp.int32, (ts, ts), 0)
    ci_ = lax.broadcasted_iota(jnp.int32, (ts, ts), 1)
    intra = (ri >= ci_) & (ri - ci_ <= ri % c)
    for j in range(hp):
        a = jnp.where(intra, _dot_nt(qg_ref[j].astype(MXU_DTYPE), kg_ref[j].astype(MXU_DTYPE)), 0.0)
        oi_ref[j] = _dot(a.astype(MXU_DTYPE), v_ref[j].astype(MXU_DTYPE))
    gain = gain_ref[...]

    def chunk(j, ci, st):
        sl = slice(ci * c, (ci + 1) * c)
        qg = qg_ref[j, sl, :].astype(MXU_DTYPE)
        vv = v_ref[j, sl, :].astype(MXU_DTYPE)
        bc = b_ref[j, sl, :]
        bl = bc[c - 1:c, :]
        o = oi_ref[j, sl, :] + _dot_nt(qg, st.astype(MXU_DTYPE))
        kd = (k_ref[j, sl, :] * jnp.exp(bl - bc)).astype(MXU_DTYPE)
        st = st * jnp.exp(bl) + _dot_tn(vv, kd)
        o = o * lax.rsqrt(jnp.mean(o * o, axis=-1, keepdims=True) + RMS_EPS) * gain
        gr = g_ref[j, sl, :]
        o_ref[0, sl, j * dk:(j + 1) * dk] = (o * (gr * _sigmoid(gr))).astype(o_ref.dtype)
        return st

    sts = [state_ref[hb * hp + j] for j in range(hp)]
    for ci in range(ts // c):
        sts = [chunk(j, ci, sts[j]) for j in range(hp)]
    for j in range(hp):
        state_ref[hb * hp + j] = sts[j]


def _hgrn(xb, w_in, lb_logits, gain, layer, ts, hp):
    bsz, seq, d = xb.shape
    dk = HGRN_DK
    nh = d // dk
    nhb = nh // hp
    ts = min(ts, seq)
    slots = lb_logits.shape[0]
    lbl = lb_logits.reshape(slots, nh, dk).transpose(1, 0, 2)
    buf = pltpu.VMEM((hp, ts, dk), F32)

    def wcols(part):
        return pl.BlockSpec((d, hp * dk), lambda b, s, h: (0, part * nhb + h))

    return pl.pallas_call(
        functools.partial(_hgrn_kernel, layer=layer, ts=ts, hp=hp),
        grid=(bsz, seq // ts, nhb),
        in_specs=[pl.BlockSpec((1, ts, d), lambda b, s, h: (b, s, 0)),
                  wcols(0), wcols(1), wcols(2), wcols(3),
                  pl.BlockSpec((hp, slots, dk), lambda b, s, h: (h, 0, 0)),
                  pl.BlockSpec((1, dk), lambda b, s, h: (0, 0))],
        out_specs=pl.BlockSpec((1, ts, hp * dk), lambda b, s, h: (b, s, h)),
        out_shape=jax.ShapeDtypeStruct((bsz, seq, d), MXU_DTYPE),
        scratch_shapes=[pltpu.VMEM((nh, dk, dk), F32), buf, buf, buf, buf, buf, buf, buf],
        compiler_params=_params(("parallel", "arbitrary", "arbitrary")),
        name="hgrn2",
    )(xb, w_in, w_in, w_in, w_in, lbl, gain.reshape(1, dk))


def _attn_kernel(q_ref, kp_ref, kc_ref, vp_ref, vc_ref, bias_ref, sink_ref, o_ref, *, group, kvp):
    kvh = pl.program_id(0)
    n = pl.program_id(2)
    blk = ATTN_BLOCK
    si = lax.broadcasted_iota(jnp.int32, (1, 2 * blk), 1)
    first = jnp.where((si >= blk) | (n > 0), 0.0, NEG_INF)
    outs = []
    for j in range(kvp):
        kk = jnp.concatenate([kp_ref[0, j], kc_ref[0, j]], axis=0)
        vv = jnp.concatenate([vp_ref[0, j], vc_ref[0, j]], axis=0)
        for g in range(group):
            hq = j * group + g
            qh = q_ref[0, :, hq * HEAD_DIM:(hq + 1) * HEAD_DIM] * ATTN_SCALE
            s = _dot_nt(qh, kk) + bias_ref[hq] + first
            sink = sink_ref[kvh * kvp * group + hq]
            m = jnp.maximum(jnp.max(s, axis=-1, keepdims=True), sink)
            p = jnp.exp(s - m)
            w = p / (jnp.sum(p, axis=-1, keepdims=True) + jnp.exp(sink - m))
            outs.append(_dot(w.astype(MXU_DTYPE), vv))
    o_ref[0] = jnp.concatenate(outs, axis=-1).astype(o_ref.dtype)


def _attention(q, k_sh, v_sh, sinks, kvp):
    bsz, seq, dq = q.shape
    group = dq // (KV_HEADS * HEAD_DIM)
    blk = ATTN_BLOCK
    gw = kvp * group * HEAD_DIM
    cur = pl.BlockSpec((1, kvp, blk, HEAD_DIM), lambda k, b, n: (b, k, n, 0))
    prev = pl.BlockSpec((1, kvp, blk, HEAD_DIM), lambda k, b, n: (b, k, jnp.maximum(n - 1, 0), 0))
    nheads = KV_HEADS * group
    slopes = jnp.asarray([2.0 ** (-8.0 * h / nheads) for h in range(1, nheads + 1)], F32)
    dist = (jnp.arange(blk)[:, None] + blk - jnp.arange(2 * blk)[None, :])
    bias = jnp.where((dist >= 0) & (dist < blk),
                     -(slopes[:, None, None] * dist.astype(F32)[None]), NEG_INF)
    return pl.pallas_call(
        functools.partial(_attn_kernel, group=group, kvp=kvp),
        grid=(KV_HEADS // kvp, bsz, seq // blk),
        in_specs=[pl.BlockSpec((1, blk, gw), lambda k, b, n: (b, n, k)),
                  prev, cur, prev, cur,
                  pl.BlockSpec((kvp * group, blk, 2 * blk), lambda k, b, n: (k, 0, 0)),
                  pl.BlockSpec(memory_space=pltpu.SMEM)],
        out_specs=pl.BlockSpec((1, blk, gw), lambda k, b, n: (b, n, k)),
        out_shape=jax.ShapeDtypeStruct((bsz, seq, dq), MXU_DTYPE),
        compiler_params=_params(("parallel", "parallel", "arbitrary")),
        name="swa_attention",
    )(q, k_sh, k_sh, v_sh, v_sh, bias, sinks)


def _sort_network(n):
    pairs = []
    p = 1
    while p < n:
        k = p
        while k >= 1:
            for j in range(k % p, n - k, 2 * k):
                for i in range(min(k, n - j - k)):
                    if (i + j) // (2 * p) == (i + j + k) // (2 * p):
                        pairs.append((i + j, i + j + k))
            k //= 2
        p *= 2
    return pairs


def _compare_exchange(v, i, j):
    if v[j] is None:
        return
    if v[i] is None:
        v[i], v[j] = v[j], None
        return
    v[i], v[j] = jnp.maximum(v[i], v[j]), jnp.minimum(v[i], v[j])


def _top16(s):
    k = PEER_TOPK
    v = [s[r * SUBLANES:(r + 1) * SUBLANES] for r in range(s.shape[0] // SUBLANES)]
    v += [None] * (k - len(v))
    for i, j in _sort_network(k):
        _compare_exchange(v, i, j)
    shift = SUBLANES // 2
    while shift >= 1:
        other = [None if a is None else pltpu.roll(a, shift, axis=0) for a in v]
        for i in range(k):
            a, b = v[i], other[k - 1 - i]
            v[i] = b if a is None else a if b is None else jnp.maximum(a, b)
        stride = k // 2
        while stride >= 1:
            for i in range(k):
                if i & stride == 0:
                    _compare_exchange(v, i, i + stride)
            stride //= 2
        shift //= 2
    return jnp.concatenate([a[0:1] for a in v], axis=0)


def _peer_sel_kernel(x_ref, wq_ref, sub_ref, s1_ref, s2_ref, st_ref, *, tb, hps):
    qd = 2 * PEER_HALF
    qall = _dot(x_ref[...].astype(MXU_DTYPE), wq_ref[...]).astype(MXU_DTYPE)
    for j in range(hps):
        q = qall[:, j * qd:(j + 1) * qd]
        s1 = _dot_nt(sub_ref[j, 0], q[:, :PEER_HALF])
        s2 = _dot_nt(sub_ref[j, 1], q[:, PEER_HALF:])
        for lb in range(tb // LANES):
            a1 = s1[:, lb * LANES:(lb + 1) * LANES]
            a2 = s2[:, lb * LANES:(lb + 1) * LANES]
            s1_ref[lb, j] = a1
            s2_ref[lb, j] = a2
            t1 = _top16(a1)
            t2 = _top16(a2)
            cands = [t1[0:1] + t2]
            for a in range(1, SUBLANES):
                cands.append(t1[a:a + 1] + t2[0:SUBLANES])
            cands.append(t1[SUBLANES:] + t2[0:1])
            cand = jnp.concatenate(cands, axis=0)
            best = _top16(cand)
            top = best[0:1]
            tau = best[PEER_TOPK - 1:PEER_TOPK]
            z = jnp.sum(jnp.where(cand >= tau, jnp.exp(cand - top), 0.0), axis=0, keepdims=True)
            zero = jnp.zeros_like(z)
            st_ref[lb, j] = jnp.concatenate(
                [tau, t1[0:1], t2[0:1], 1.0 / z, zero, zero, zero, zero], axis=0)


def _peer_select(x, w_q, subkeys, tb, hps):
    t, d = x.shape
    tb = min(tb, t)
    nlb = tb // LANES
    qd = 2 * PEER_HALF
    blk = lambda rows: pl.BlockSpec((nlb, hps, rows, LANES), lambda i, h: (i, h, 0, 0))
    shp = lambda rows: jax.ShapeDtypeStruct((t // LANES, PEER_HEADS, rows, LANES), F32)
    return pl.pallas_call(
        functools.partial(_peer_sel_kernel, tb=tb, hps=hps),
        grid=(t // tb, PEER_HEADS // hps),
        in_specs=[pl.BlockSpec((tb, d), lambda i, h: (i, 0)),
                  pl.BlockSpec((d, hps * qd), lambda i, h: (0, h)),
                  pl.BlockSpec((hps, 2, PEER_NKEYS, PEER_HALF), lambda i, h: (h, 0, 0, 0))],
        out_specs=[blk(PEER_NKEYS), blk(PEER_NKEYS), blk(SUBLANES)],
        out_shape=[shp(PEER_NKEYS), shp(PEER_NKEYS), shp(SUBLANES)],
        compiler_params=_params(("parallel", "arbitrary")),
        name="peer_select",
    )(x, w_q, subkeys)


def _gelu(x):
    return 0.5 * x * (1.0 + lax.erf(x * (1.0 / math.sqrt(2.0))))


def _peer_dense_kernel(x_ref, u_ref, v_ref, s1_ref, s2_ref, st_ref, o_ref,
                       e2_ref, w1_ref, g_ref, at_ref, *, tb, ec):
    e = pl.program_id(1)
    nlb = tb // LANES
    nsub = ec // PEER_NKEYS

    @pl.when(e == 0)
    def _():
        o_ref[...] = jnp.zeros(o_ref.shape, F32)
        for lb in range(nlb):
            for h in range(PEER_HEADS):
                e2_ref[lb, h] = jnp.exp(s2_ref[lb, h] - st_ref[lb, h, 2:3, :])
                w1_ref[lb, h] = jnp.exp(s1_ref[lb, h] - st_ref[lb, h, 1:2, :]) * st_ref[lb, h, 3:4, :]

    for ii in range(nsub):
        i = e * nsub + ii
        for lb in range(nlb):
            lanes = slice(lb * LANES, (lb + 1) * LANES)
            for kt in range(PEER_NKEYS // GATE_ROWS):
                keys = slice(kt * GATE_ROWS, (kt + 1) * GATE_ROWS)
                gate = jnp.zeros((GATE_ROWS, LANES), F32)
                for h in range(PEER_HEADS):
                    tau = st_ref[lb, h, 0:1, :]
                    cs = s1_ref[lb, h, pl.ds(i, 1), :] + s2_ref[lb, h, keys, :]
                    val = e2_ref[lb, h, keys, :] * w1_ref[lb, h, pl.ds(i, 1), :]
                    gate = gate + jnp.where(cs >= tau, val, 0.0)
                g_ref[ii * PEER_NKEYS + kt * GATE_ROWS:ii * PEER_NKEYS + (kt + 1) * GATE_ROWS,
                      lanes] = gate

    ht = _dot_nt(u_ref[...], x_ref[...])
    at_ref[...] = (_gelu(ht) * g_ref[...]).astype(at_ref.dtype)
    o_ref[...] += _dot_tn(at_ref[...], v_ref[...])


def _peer_dense(x, u, v, layer, s1, s2, st, tb, ec):
    t, d = x.shape
    tb = min(tb, t)
    nlb = tb // LANES
    ne = u.shape[1] // ec
    once = pl.Buffered(1)
    sblk = lambda rows: pl.BlockSpec((nlb, PEER_HEADS, rows, LANES), lambda i, e: (i, 0, 0, 0),
                                     pipeline_mode=once)
    return pl.pallas_call(
        functools.partial(_peer_dense_kernel, tb=tb, ec=ec),
        grid=(t // tb, ne),
        in_specs=[pl.BlockSpec((tb, d), lambda i, e: (i, 0), pipeline_mode=once),
                  pl.BlockSpec((None, ec, d), lambda i, e: (layer, e, 0)),
                  pl.BlockSpec((None, ec, d), lambda i, e: (layer, e, 0)),
                  sblk(PEER_NKEYS), sblk(PEER_NKEYS), sblk(SUBLANES)],
        out_specs=pl.BlockSpec((tb, d), lambda i, e: (i, 0), pipeline_mode=once),
        out_shape=jax.ShapeDtypeStruct((t, d), F32),
        scratch_shapes=[pltpu.VMEM((nlb, PEER_HEADS, PEER_NKEYS, LANES), F32),
                        pltpu.VMEM((nlb, PEER_HEADS, PEER_NKEYS, LANES), F32),
                        pltpu.VMEM((ec, tb), F32),
                        pltpu.VMEM((ec, tb), MXU_DTYPE)],
        compiler_params=_params(("parallel", "arbitrary")),
        name="peer_dense",
    )(x, u, v, s1, s2, st)


def _peer_layer(x, xb, w_q, subkeys, u, v, layer, gain, bias, with_copy):
    s1, s2, st = _peer_select(xb, w_q, subkeys, tb=512, hps=2)
    y = _peer_dense(xb, u, v, layer, s1, s2, st, tb=512, ec=1024)
    return _res_ln(x, y, gain, bias, tm=256, with_copy=with_copy)


def kernel(x, hgrn_w_in, hgrn_lb_logits, hgrn_norm_gain, hgrn_w_out, kv_w, attn_w_q, attn_sinks,
           attn_w_out, peer_w_q, peer_subkeys, peer_u, peer_v, ln_gain, ln_bias):
    bsz, seq, d = x.shape
    t = bsz * seq
    cdt = MXU_DTYPE
    dk = HGRN_DK
    nh = d // dk

    o = _hgrn(x.astype(cdt), hgrn_w_in[0].astype(cdt), hgrn_lb_logits, hgrn_norm_gain[0],
              layer=0, ts=512, hp=4)
    xt = x.reshape(t, d)
    x1, x1b = _matmul_ln(o.reshape(t, d), hgrn_w_out[0].astype(cdt), xt, ln_gain[0, 0], ln_bias[0, 0],
                         tm=512, tn=1024)
    u_all = peer_u.astype(cdt)
    v_all = peer_v.astype(cdt)
    x2, x2b = _peer_layer(x1, x1b, peer_w_q[0].astype(cdt), peer_subkeys[0].astype(cdt),
                          u_all, v_all, 0, ln_gain[0, 1], ln_bias[0, 1], with_copy=True)

    k_sh, v_sh = _kv_proj(x2b.reshape(bsz, seq, d), kv_w.astype(cdt), tm=512)

    q = _matmul(x2b, attn_w_q[0].astype(cdt), cdt, tm=512, tn=1024)
    att = _attention(q.reshape(bsz, seq, -1), k_sh, v_sh, attn_sinks[0].astype(F32), kvp=4)
    x3, x3b = _matmul_ln(att.reshape(t, -1), attn_w_out[0].astype(cdt), x2, ln_gain[1, 0], ln_bias[1, 0],
                         tm=512, tn=1024)
    (x4,) = _peer_layer(x3, x3b, peer_w_q[1].astype(cdt), peer_subkeys[1].astype(cdt),
                        u_all, v_all, 1, ln_gain[1, 1], ln_bias[1, 1], with_copy=False)
    return x4.reshape(bsz, seq, d)
```

```python
import functools
import math

import jax
import jax.numpy as jnp
from jax import lax
from jax.experimental import pallas as pl
from jax.experimental.pallas import tpu as pltpu

F32 = jnp.float32
MXU_DTYPE = jnp.bfloat16

DEPTH = 2
ALPHA = (2.0 * DEPTH) ** 0.25
LN_EPS = 1e-5
LN_UNROLL = 4
RMS_EPS = 1e-6

HGRN_DK = 128
HGRN_CHUNK = 32

HEAD_DIM = 64
KV_HEADS = 8
ATTN_BLOCK = 128
ATTN_SCALE = HEAD_DIM ** -0.5
NEG_INF = -1e30

PEER_HEADS = 8
PEER_NKEYS = 128
PEER_TOPK = 16
PEER_HALF = 128
GATE_ROWS = 32

LANES = 128
SUBLANES = 8
VMEM_LIMIT = 60 * 1024 * 1024


def _params(sem, flags=None):
    return pltpu.CompilerParams(dimension_semantics=sem, vmem_limit_bytes=VMEM_LIMIT, flags=flags)


def _sigmoid(x):
    return 1.0 / (1.0 + jnp.exp(-x))


def _dot(a, b):
    return jnp.dot(a, b, preferred_element_type=F32)


def _dot_nt(a, b):
    return lax.dot_general(a, b, (((1,), (1,)), ((), ())), preferred_element_type=F32)


def _dot_tn(a, b):
    return lax.dot_general(a, b, (((0,), (0,)), ((), ())), preferred_element_type=F32)


def _mm_kernel(a_ref, w_ref, o_ref):
    a = a_ref[...].astype(MXU_DTYPE)
    o_ref[...] = _dot(a, w_ref[...]).astype(o_ref.dtype)


def _matmul(a, w, out_dtype, tm, tn):
    m, k = a.shape
    n = w.shape[1]
    tm, tn = min(tm, m), min(tn, n)
    return pl.pallas_call(
        _mm_kernel,
        grid=(m // tm, n // tn),
        in_specs=[pl.BlockSpec((tm, k), lambda i, j: (i, 0)),
                  pl.BlockSpec((k, tn), lambda i, j: (0, j))],
        out_specs=pl.BlockSpec((tm, tn), lambda i, j: (i, j)),
        out_shape=jax.ShapeDtypeStruct((m, n), out_dtype),
        compiler_params=_params(("parallel", "arbitrary")),
        name="matmul",
    )(a, w)


def _kv_kernel(a_ref, w_ref, k_ref, v_ref):
    y = _dot(a_ref[0].astype(MXU_DTYPE), w_ref[...])
    nkv = KV_HEADS * HEAD_DIM
    for h in range(KV_HEADS):
        cols = slice(h * HEAD_DIM, (h + 1) * HEAD_DIM)
        k_ref[0, h] = y[:, cols].astype(k_ref.dtype)
        v_ref[0, h] = y[:, nkv:][:, cols].astype(v_ref.dtype)


def _kv_proj(x, w, tm):
    bsz, seq, d = x.shape
    tm = min(tm, seq)
    out = pl.BlockSpec((1, KV_HEADS, tm, HEAD_DIM), lambda b, i: (b, 0, i, 0))
    shape = jax.ShapeDtypeStruct((bsz, KV_HEADS, seq, HEAD_DIM), MXU_DTYPE)
    return pl.pallas_call(
        _kv_kernel,
        grid=(bsz, seq // tm),
        in_specs=[pl.BlockSpec((1, tm, d), lambda b, i: (b, i, 0)),
                  pl.BlockSpec(w.shape, lambda b, i: (0, 0))],
        out_specs=[out, out],
        out_shape=[shape, shape],
        compiler_params=_params(("parallel", "arbitrary")),
        name="kv_proj",
    )(x, w)


def _row_mean(z):
    n = z.shape[-1]
    return jnp.sum(_tree_sum(_lane_groups(z)), axis=-1, keepdims=True) * (1.0 / n)


def _lane_groups(z):
    return [z[:, k * LANES:(k + 1) * LANES] for k in range(z.shape[-1] // LANES)]


def _tree_sum(parts):
    while len(parts) > 1:
        parts = [a + b for a, b in zip(parts[0::2], parts[1::2])] + parts[len(parts) & ~1:]
    return parts[0]


def _ln_rows(z, g, b):
    zc = z - _row_mean(z)
    var = _row_mean(zc * zc)
    return zc * lax.rsqrt(var + LN_EPS) * g + b


def _mm_ln_kernel(a_ref, w_ref, res_ref, g_ref, b_ref, o_ref, ob_ref, mu_ref, rs_ref,
                  *, tn, nj, rows):
    j = pl.program_id(1)
    y = _dot(a_ref[...].astype(MXU_DTYPE), w_ref[...]) + ALPHA * res_ref[...]
    for jj in range(nj):
        @pl.when(j == jj)
        def _(jj=jj):
            o_ref[:, jj * tn:(jj + 1) * tn] = y

    @pl.when(j == nj - 1)
    def _():
        ngroups = o_ref.shape[0] // rows
        n = o_ref.shape[1]
        group = lambda r: pl.ds(pl.multiple_of(r * rows, rows), rows)

        def mean_pass(r, carry):
            sl = group(r)
            mu_ref[sl, :] = jnp.broadcast_to(_row_mean(o_ref[sl, :]), (rows, LANES))
            return carry

        def var_pass(r, carry):
            sl = group(r)
            mu = mu_ref[sl, :]
            sq = [(zk - mu) * (zk - mu) for zk in _lane_groups(o_ref[sl, :])]
            var = jnp.sum(_tree_sum(sq), axis=-1, keepdims=True) * (1.0 / n)
            rs_ref[sl, :] = jnp.broadcast_to(lax.rsqrt(var + LN_EPS), (rows, LANES))
            return carry

        def norm_pass(r, carry):
            sl = group(r)
            mu = mu_ref[sl, :]
            rs = rs_ref[sl, :]
            for k, zk in enumerate(_lane_groups(o_ref[sl, :])):
                cols = slice(k * LANES, (k + 1) * LANES)
                out = (zk - mu) * rs * g_ref[:, cols] + b_ref[:, cols]
                o_ref[sl, cols] = out
                ob_ref[sl, cols] = out.astype(ob_ref.dtype)
            return carry

        lax.fori_loop(0, ngroups, mean_pass, 0, unroll=2 * LN_UNROLL)
        lax.fori_loop(0, ngroups, var_pass, 0, unroll=2 * LN_UNROLL)
        lax.fori_loop(0, ngroups, norm_pass, 0, unroll=LN_UNROLL)


def _matmul_ln(a, w, res, gain, bias, tm, tn):
    m, k = a.shape
    n = w.shape[1]
    tm, tn = min(tm, m), min(tn, n)
    nj = n // tn
    full = pl.BlockSpec((tm, n), lambda i, j: (i, 0))
    return pl.pallas_call(
        functools.partial(_mm_ln_kernel, tn=tn, nj=nj, rows=SUBLANES),
        grid=(m // tm, nj),
        in_specs=[pl.BlockSpec((tm, k), lambda i, j: (i, 0)),
                  pl.BlockSpec((k, tn), lambda i, j: (0, j)),
                  pl.BlockSpec((tm, tn), lambda i, j: (i, j)),
                  pl.BlockSpec((1, n), lambda i, j: (0, 0)),
                  pl.BlockSpec((1, n), lambda i, j: (0, 0))],
        out_specs=[full, full],
        out_shape=[jax.ShapeDtypeStruct((m, n), F32), jax.ShapeDtypeStruct((m, n), MXU_DTYPE)],
        scratch_shapes=[pltpu.VMEM((tm, LANES), F32), pltpu.VMEM((tm, LANES), F32)],
        compiler_params=_params(("parallel", "arbitrary")),
        name="matmul_ln",
    )(a, w, res, gain.reshape(1, n), bias.reshape(1, n))


def _res_ln_kernel(x_ref, y_ref, g_ref, b_ref, *o_refs):
    z = _ln_rows(ALPHA * x_ref[...] + y_ref[...], g_ref[...], b_ref[...])
    for o_ref in o_refs:
        o_ref[...] = z.astype(o_ref.dtype)


def _res_ln(x, y, gain, bias, tm, with_copy):
    m, n = x.shape
    tm = min(tm, m)
    row = pl.BlockSpec((tm, n), lambda i: (i, 0))
    vec = pl.BlockSpec((1, n), lambda i: (0, 0))
    dtypes = (F32, MXU_DTYPE) if with_copy else (F32,)
    return pl.pallas_call(
        _res_ln_kernel,
        grid=(m // tm,),
        in_specs=[row, row, vec, vec],
        out_specs=[row] * len(dtypes),
        out_shape=[jax.ShapeDtypeStruct((m, n), dt) for dt in dtypes],
        compiler_params=_params(("parallel",)),
        name="res_ln",
    )(x, y, gain.reshape(1, n), bias.reshape(1, n))


def _hgrn_kernel(x_ref, wq_ref, wf_ref, wi_ref, wg_ref, lbl_ref, gain_ref, o_ref,
                 state_ref, qg_ref, kg_ref, k_ref, b_ref, v_ref, g_ref, oi_ref, *, layer, ts, hp):
    hb = pl.program_id(1)
    sblk = pl.program_id(2)
    dk = HGRN_DK
    c = HGRN_CHUNK

    @pl.when(sblk == 0)
    def _():
        for j in range(hp):
            state_ref[hb * hp + j] = jnp.zeros((dk, dk), F32)

    x = x_ref[0]
    pq, pf, pi, pg = (_dot(x, w[...]) for w in (wq_ref, wf_ref, wi_ref, wg_ref))
    row = lax.broadcasted_iota(jnp.int32, (ts, dk), 0) % c
    for j in range(hp):
        cols = slice(j * dk, (j + 1) * dk)
        lg = lbl_ref[j]
        ex = jnp.exp(lg - jnp.max(lg, axis=0, keepdims=True))
        lb = jnp.sum(ex[:layer + 1], axis=0, keepdims=True) / jnp.sum(ex, axis=0, keepdims=True)

        q_raw = pq[:, cols]
        f_raw = pf[:, cols]
        f = lb + (1.0 - lb) * _sigmoid(f_raw)
        k = 1.0 - f
        b = jnp.log(f)
        sh = 1
        while sh < c:
            b = b + jnp.where(row >= sh, pltpu.roll(b, sh, axis=0), 0.0)
            sh *= 2
        qg_ref[j] = q_raw * _sigmoid(q_raw) * jnp.exp(b)
        kg_ref[j] = k * jnp.exp(-b)
        k_ref[j] = k
        b_ref[j] = b
        v_ref[j] = pi[:, cols]
        g_ref[j] = pg[:, cols]

    ri = lax.broadcasted_iota(jnp.int32, (ts, ts), 0)
    ci_ = lax.broadcasted_iota(jnp.int32, (ts, ts), 1)
    intra = (ri >= ci_) & (ri - ci_ <= ri % c)
    for j in range(hp):
        a = jnp.where(intra, _dot_nt(qg_ref[j].astype(MXU_DTYPE), kg_ref[j].astype(MXU_DTYPE)), 0.0)
        oi_ref[j] = _dot(a.astype(MXU_DTYPE), v_ref[j].astype(MXU_DTYPE))
    gain = gain_ref[...]

    def chunk(j, ci, st):
        sl = slice(ci * c, (ci + 1) * c)
        qg = qg_ref[j, sl, :].astype(MXU_DTYPE)
        vv = v_ref[j, sl, :].astype(MXU_DTYPE)
        bc = b_ref[j, sl, :]
        bl = bc[c - 1:c, :]
        o = oi_ref[j, sl, :] + _dot_nt(qg, st.astype(MXU_DTYPE))
        kd = (k_ref[j, sl, :] * jnp.exp(bl - bc)).astype(MXU_DTYPE)
        st = st * jnp.exp(bl) + _dot_tn(vv, kd)
        o = o * lax.rsqrt(jnp.mean(o * o, axis=-1, keepdims=True) + RMS_EPS) * gain
        gr = g_ref[j, sl, :]
        o_ref[0, sl, j * dk:(j + 1) * dk] = (o * (gr * _sigmoid(gr))).astype(o_ref.dtype)
        return st

    sts = [state_ref[hb * hp + j] for j in range(hp)]
    for ci in range(ts // c):
        sts = [chunk(j, ci, sts[j]) for j in range(hp)]
    for j in range(hp):
        state_ref[hb * hp + j] = sts[j]


def _hgrn(xb, w_in, lb_logits, gain, layer, ts, hp):
    bsz, seq, d = xb.shape
    dk = HGRN_DK
    nh = d // dk
    nhb = nh // hp
    ts = min(ts, seq)
    slots = lb_logits.shape[0]
    lbl = lb_logits.reshape(slots, nh, dk).transpose(1, 0, 2)
    buf = pltpu.VMEM((hp, ts, dk), F32)

    def wcols(part):
        return pl.BlockSpec((d, hp * dk), lambda b, h, s: (0, part * nhb + h))

    return pl.pallas_call(
        functools.partial(_hgrn_kernel, layer=layer, ts=ts, hp=hp),
        grid=(bsz, nhb, seq // ts),
        in_specs=[pl.BlockSpec((1, ts, d), lambda b, h, s: (b, s, 0)),
                  wcols(0), wcols(1), wcols(2), wcols(3),
                  pl.BlockSpec((hp, slots, dk), lambda b, h, s: (h, 0, 0)),
                  pl.BlockSpec((1, dk), lambda b, h, s: (0, 0))],
        out_specs=pl.BlockSpec((1, ts, hp * dk), lambda b, h, s: (b, s, h)),
        out_shape=jax.ShapeDtypeStruct((bsz, seq, d), MXU_DTYPE),
        scratch_shapes=[pltpu.VMEM((nh, dk, dk), F32), buf, buf, buf, buf, buf, buf, buf],
        compiler_params=_params(("parallel", "arbitrary", "arbitrary")),
        name="hgrn2",
    )(xb, w_in, w_in, w_in, w_in, lbl, gain.reshape(1, dk))


def _attn_kernel(q_ref, kp_ref, kc_ref, vp_ref, vc_ref, bias_ref, sink_ref, o_ref, *, group, kvp):
    kvh = pl.program_id(0)
    n = pl.program_id(2)
    blk = ATTN_BLOCK
    si = lax.broadcasted_iota(jnp.int32, (1, 2 * blk), 1)
    first = jnp.where((si >= blk) | (n > 0), 0.0, NEG_INF)
    outs = []
    for j in range(kvp):
        kk = jnp.concatenate([kp_ref[0, j], kc_ref[0, j]], axis=0)
        vv = jnp.concatenate([vp_ref[0, j], vc_ref[0, j]], axis=0)
        for g in range(group):
            hq = j * group + g
            qh = q_ref[0, :, hq * HEAD_DIM:(hq + 1) * HEAD_DIM] * ATTN_SCALE
            s = _dot_nt(qh, kk) + bias_ref[hq] + first
            sink = sink_ref[kvh * kvp * group + hq]
            m = jnp.maximum(jnp.max(s, axis=-1, keepdims=True), sink)
            p = jnp.exp(s - m)
            w = p / (jnp.sum(p, axis=-1, keepdims=True) + jnp.exp(sink - m))
            outs.append(_dot(w.astype(MXU_DTYPE), vv))
    o_ref[0] = jnp.concatenate(outs, axis=-1).astype(o_ref.dtype)


def _attention(q, k_sh, v_sh, sinks, kvp):
    bsz, seq, dq = q.shape
    group = dq // (KV_HEADS * HEAD_DIM)
    blk = ATTN_BLOCK
    gw = kvp * group * HEAD_DIM
    cur = pl.BlockSpec((1, kvp, blk, HEAD_DIM), lambda k, b, n: (b, k, n, 0))
    prev = pl.BlockSpec((1, kvp, blk, HEAD_DIM), lambda k, b, n: (b, k, jnp.maximum(n - 1, 0), 0))
    nheads = KV_HEADS * group
    slopes = jnp.asarray([2.0 ** (-8.0 * h / nheads) for h in range(1, nheads + 1)], F32)
    dist = (jnp.arange(blk)[:, None] + blk - jnp.arange(2 * blk)[None, :])
    bias = jnp.where((dist >= 0) & (dist < blk),
                     -(slopes[:, None, None] * dist.astype(F32)[None]), NEG_INF)
    return pl.pallas_call(
        functools.partial(_attn_kernel, group=group, kvp=kvp),
        grid=(KV_HEADS // kvp, bsz, seq // blk),
        in_specs=[pl.BlockSpec((1, blk, gw), lambda k, b, n: (b, n, k)),
                  prev, cur, prev, cur,
                  pl.BlockSpec((kvp * group, blk, 2 * blk), lambda k, b, n: (k, 0, 0)),
                  pl.BlockSpec(memory_space=pltpu.SMEM)],
        out_specs=pl.BlockSpec((1, blk, gw), lambda k, b, n: (b, n, k)),
        out_shape=jax.ShapeDtypeStruct((bsz, seq, dq), MXU_DTYPE),
        compiler_params=_params(("parallel", "parallel", "arbitrary")),
        name="swa_attention",
    )(q, k_sh, k_sh, v_sh, v_sh, bias, sinks)


def _sort_network(n):
    pairs = []
    p = 1
    while p < n:
        k = p
        while k >= 1:
            for j in range(k % p, n - k, 2 * k):
                for i in range(min(k, n - j - k)):
                    if (i + j) // (2 * p) == (i + j + k) // (2 * p):
                        pairs.append((i + j, i + j + k))
            k //= 2
        p *= 2
    return pairs


def _compare_exchange(v, i, j):
    if v[j] is None:
        return
    if v[i] is None:
        v[i], v[j] = v[j], None
        return
    v[i], v[j] = jnp.maximum(v[i], v[j]), jnp.minimum(v[i], v[j])


def _top16(s):
    k = PEER_TOPK
    v = [s[r * SUBLANES:(r + 1) * SUBLANES] for r in range(s.shape[0] // SUBLANES)]
    v += [None] * (k - len(v))
    for i, j in _sort_network(k):
        _compare_exchange(v, i, j)
    shift = SUBLANES // 2
    while shift >= 1:
        other = [None if a is None else pltpu.roll(a, shift, axis=0) for a in v]
        for i in range(k):
            a, b = v[i], other[k - 1 - i]
            v[i] = b if a is None else a if b is None else jnp.maximum(a, b)
        stride = k // 2
        while stride >= 1:
            for i in range(k):
                if i & stride == 0:
                    _compare_exchange(v, i, i + stride)
            stride //= 2
        shift //= 2
    return jnp.concatenate([a[0:1] for a in v], axis=0)


def _peer_sel_kernel(x_ref, wq_ref, sub_ref, s1_ref, s2_ref, st_ref, *, tb, hps):
    qd = 2 * PEER_HALF
    qall = _dot(x_ref[...].astype(MXU_DTYPE), wq_ref[...]).astype(MXU_DTYPE)
    for j in range(hps):
        q = qall[:, j * qd:(j + 1) * qd]
        s1 = _dot_nt(sub_ref[j, 0], q[:, :PEER_HALF])
        s2 = _dot_nt(sub_ref[j, 1], q[:, PEER_HALF:])
        for lb in range(tb // LANES):
            a1 = s1[:, lb * LANES:(lb + 1) * LANES]
            a2 = s2[:, lb * LANES:(lb + 1) * LANES]
            s1_ref[lb, j] = a1
            s2_ref[lb, j] = a2
            t1 = _top16(a1)
            t2 = _top16(a2)
            cands = [t1[0:1] + t2]
            for a in range(1, SUBLANES):
                cands.append(t1[a:a + 1] + t2[0:SUBLANES])
            cands.append(t1[SUBLANES:] + t2[0:1])
            cand = jnp.concatenate(cands, axis=0)
            best = _top16(cand)
            top = best[0:1]
            tau = best[PEER_TOPK - 1:PEER_TOPK]
            z = jnp.sum(jnp.where(cand >= tau, jnp.exp(cand - top), 0.0), axis=0, keepdims=True)
            zero = jnp.zeros_like(z)
            st_ref[lb, j] = jnp.concatenate(
                [tau, t1[0:1], t2[0:1], 1.0 / z, zero, zero, zero, zero], axis=0)


def _peer_select(x, w_q, subkeys, tb, hps):
    t, d = x.shape
    tb = min(tb, t)
    nlb = tb // LANES
    qd = 2 * PEER_HALF
    blk = lambda rows: pl.BlockSpec((nlb, hps, rows, LANES), lambda i, h: (i, h, 0, 0))
    shp = lambda rows: jax.ShapeDtypeStruct((t // LANES, PEER_HEADS, rows, LANES), F32)
    return pl.pallas_call(
        functools.partial(_peer_sel_kernel, tb=tb, hps=hps),
        grid=(t // tb, PEER_HEADS // hps),
        in_specs=[pl.BlockSpec((tb, d), lambda i, h: (i, 0)),
                  pl.BlockSpec((d, hps * qd), lambda i, h: (0, h)),
                  pl.BlockSpec((hps, 2, PEER_NKEYS, PEER_HALF), lambda i, h: (h, 0, 0, 0))],
        out_specs=[blk(PEER_NKEYS), blk(PEER_NKEYS), blk(SUBLANES)],
        out_shape=[shp(PEER_NKEYS), shp(PEER_NKEYS), shp(SUBLANES)],
        compiler_params=_params(("parallel", "arbitrary")),
        name="peer_select",
    )(x, w_q, subkeys)


def _gelu(x):
    return 0.5 * x * (1.0 + lax.erf(x * (1.0 / math.sqrt(2.0))))


def _peer_dense_kernel(x_ref, u_ref, v_ref, s1_ref, s2_ref, st_ref, o_ref,
                       e2_ref, w1_ref, g_ref, at_ref, *, tb, ec):
    e = pl.program_id(1)
    nlb = tb // LANES
    nsub = ec // PEER_NKEYS

    @pl.when(e == 0)
    def _():
        o_ref[...] = jnp.zeros(o_ref.shape, F32)
        for lb in range(nlb):
            for h in range(PEER_HEADS):
                e2_ref[lb, h] = jnp.exp(s2_ref[lb, h] - st_ref[lb, h, 2:3, :])
                w1_ref[lb, h] = jnp.exp(s1_ref[lb, h] - st_ref[lb, h, 1:2, :]) * st_ref[lb, h, 3:4, :]

    for ii in range(nsub):
        i = e * nsub + ii
        for lb in range(nlb):
            lanes = slice(lb * LANES, (lb + 1) * LANES)
            for kt in range(PEER_NKEYS // GATE_ROWS):
                keys = slice(kt * GATE_ROWS, (kt + 1) * GATE_ROWS)
                gate = jnp.zeros((GATE_ROWS, LANES), F32)
                for h in range(PEER_HEADS):
                    tau = st_ref[lb, h, 0:1, :]
                    cs = s1_ref[lb, h, pl.ds(i, 1), :] + s2_ref[lb, h, keys, :]
                    val = e2_ref[lb, h, keys, :] * w1_ref[lb, h, pl.ds(i, 1), :]
                    gate = gate + jnp.where(cs >= tau, val, 0.0)
                g_ref[ii * PEER_NKEYS + kt * GATE_ROWS:ii * PEER_NKEYS + (kt + 1) * GATE_ROWS,
                      lanes] = gate

    ht = _dot_nt(u_ref[...], x_ref[...])
    at_ref[...] = (_gelu(ht) * g_ref[...]).astype(at_ref.dtype)
    o_ref[...] += _dot_tn(at_ref[...], v_ref[...])


def _peer_dense(x, u, v, layer, s1, s2, st, tb, ec):
    t, d = x.shape
    tb = min(tb, t)
    nlb = tb // LANES
    ne = u.shape[1] // ec
    once = pl.Buffered(1)
    sblk = lambda rows: pl.BlockSpec((nlb, PEER_HEADS, rows, LANES), lambda i, e: (i, 0, 0, 0),
                                     pipeline_mode=once)
    return pl.pallas_call(
        functools.partial(_peer_dense_kernel, tb=tb, ec=ec),
        grid=(t // tb, ne),
        in_specs=[pl.BlockSpec((tb, d), lambda i, e: (i, 0), pipeline_mode=once),
                  pl.BlockSpec((None, ec, d), lambda i, e: (layer, e, 0)),
                  pl.BlockSpec((None, ec, d), lambda i, e: (layer, e, 0)),
                  sblk(PEER_NKEYS), sblk(PEER_NKEYS), sblk(SUBLANES)],
        out_specs=pl.BlockSpec((tb, d), lambda i, e: (i, 0), pipeline_mode=once),
        out_shape=jax.ShapeDtypeStruct((t, d), F32),
        scratch_shapes=[pltpu.VMEM((nlb, PEER_HEADS, PEER_NKEYS, LANES), F32),
                        pltpu.VMEM((nlb, PEER_HEADS, PEER_NKEYS, LANES), F32),
                        pltpu.VMEM((ec, tb), F32),
                        pltpu.VMEM((ec, tb), MXU_DTYPE)],
        compiler_params=_params(("parallel", "arbitrary")),
        name="peer_dense",
    )(x, u, v, s1, s2, st)


def _peer_layer(x, xb, w_q, subkeys, u, v, layer, gain, bias, with_copy):
    s1, s2, st = _peer_select(xb, w_q, subkeys, tb=512, hps=2)
    y = _peer_dense(xb, u, v, layer, s1, s2, st, tb=512, ec=1024)
    return _res_ln(x, y, gain, bias, tm=256, with_copy=with_copy)


def kernel(x, hgrn_w_in, hgrn_lb_logits, hgrn_norm_gain, hgrn_w_out, kv_w, attn_w_q, attn_sinks,
           attn_w_out, peer_w_q, peer_subkeys, peer_u, peer_v, ln_gain, ln_bias):
    bsz, seq, d = x.shape
    t = bsz * seq
    cdt = MXU_DTYPE
    dk = HGRN_DK
    nh = d // dk

    o = _hgrn(x.astype(cdt), hgrn_w_in[0].astype(cdt), hgrn_lb_logits, hgrn_norm_gain[0],
              layer=0, ts=512, hp=4)
    xt = x.reshape(t, d)
    x1, x1b = _matmul_ln(o.reshape(t, d), hgrn_w_out[0].astype(cdt), xt, ln_gain[0, 0], ln_bias[0, 0],
                         tm=512, tn=1024)
    u_all = peer_u.astype(cdt)
    v_all = peer_v.astype(cdt)
    x2, x2b = _peer_layer(x1, x1b, peer_w_q[0].astype(cdt), peer_subkeys[0].astype(cdt),
                          u_all, v_all, 0, ln_gain[0, 1], ln_bias[0, 1], with_copy=True)

    k_sh, v_sh = _kv_proj(x2b.reshape(bsz, seq, d), kv_w.astype(cdt), tm=512)

    q = _matmul(x2b, attn_w_q[0].astype(cdt), cdt, tm=512, tn=1024)
    att = _attention(q.reshape(bsz, seq, -1), k_sh, v_sh, attn_sinks[0].astype(F32), kvp=4)
    x3, x3b = _matmul_ln(att.reshape(t, -1), attn_w_out[0].astype(cdt), x2, ln_gain[1, 0], ln_bias[1, 0],
                         tm=512, tn=1024)
    (x4,) = _peer_layer(x3, x3b, peer_w_q[1].astype(cdt), peer_subkeys[1].astype(cdt),
                        u_all, v_all, 1, ln_gain[1, 1], ln_bias[1, 1], with_copy=False)
    return x4.reshape(bsz, seq, d)
```
